```python
import math
import jax
import jax.numpy as jnp
from jax import lax
import numpy as np

D_MODEL = 2048
BATCH = 4
SEQ = 4096
DEPTH = 2

GRID_W = 64
CTX_LEN = 256
EPS = 1e-6
N_MOD = 6
D_FF = -(-8 * D_MODEL // (3 * 256)) * 256
N_EVEN = (DEPTH + 1) // 2
N_ODD = DEPTH // 2
A_HEADS = D_MODEL // 256
A_QK_DIM = 64
A_V_DIM = 2 * A_QK_DIM
ROPE_HALF = A_QK_DIM // 2
ROPE_FREQS = ROPE_HALF // 2
ROPE_BASE = 10000.0
Q_BLOCK = 128
B_GROUPS = D_MODEL // 256
B_CH = 128
CHUNK = 128
C_CH = D_MODEL // 2
CONV_W = 31
D_GROUPS = D_MODEL // 256
D_CH = 128
A_Q_COLS = A_HEADS * 2 * A_QK_DIM
A_V_COLS = A_HEADS * A_V_DIM
B_COLS = B_GROUPS * B_CH
AB_IN = 2 * A_Q_COLS + A_V_COLS + 2 * B_COLS
AB_MIX = A_V_COLS + B_COLS
D_COLS = D_GROUPS * D_CH
CD_IN = 2 * C_CH + D_COLS
CD_MIX = C_CH + D_COLS

kernel_name = 'hybrid_diffattn_gmlp_conformer_fnet_dit'


def rms_norm(x, g=None):
    xf = x.astype(jnp.float32)
    y = xf * lax.rsqrt(jnp.mean(xf * xf, axis=-1, keepdims=True) + EPS)
    if g is not None:
        y = y * g.astype(jnp.float32)
    return y.astype(x.dtype)


def layer_norm(x, g, b):
    xf = x.astype(jnp.float32)
    mu = jnp.mean(xf, axis=-1, keepdims=True)
    var = jnp.mean(jnp.square(xf - mu), axis=-1, keepdims=True)
    y = (xf - mu) * lax.rsqrt(var + EPS) * g.astype(jnp.float32) + b.astype(jnp.float32)
    return y.astype(x.dtype)


def modulate(h, shift, scale):
    return h * (1 + scale) + shift


def grid_positions(n):
    n_rows = n // GRID_W
    rows = jnp.repeat(jnp.arange(n_rows, dtype=jnp.float32), GRID_W)
    cols = jnp.tile(jnp.arange(GRID_W, dtype=jnp.float32), n_rows)
    return rows, cols


def _rotate(t, ang):
    cos = jnp.cos(ang).astype(t.dtype)[:, None, None, :]
    sin = jnp.sin(ang).astype(t.dtype)[:, None, None, :]
    t1, t2 = t[..., :ROPE_FREQS], t[..., ROPE_FREQS:]
    return jnp.concatenate([t1 * cos - t2 * sin, t1 * sin + t2 * cos], axis=-1)


def axial_rope(t, rows, cols):
    inv = ROPE_BASE ** (-jnp.arange(ROPE_FREQS, dtype=jnp.float32) / ROPE_FREQS)
    return jnp.concatenate([_rotate(t[..., :ROPE_HALF], rows[:, None] * inv),
                            _rotate(t[..., ROPE_HALF:], cols[:, None] * inv)], axis=-1)


def diff_softmax_attend(q, k, v, lam):
    s = jnp.einsum('bqhtd,bkhtd->bhtqk', q, k).astype(jnp.float32) * (A_QK_DIM ** -0.5)
    p = jax.nn.softmax(s, axis=-1)
    a = p[:, :, 0] - lam * p[:, :, 1]
    return jnp.einsum('bhqk,bkhd->bqhd', a.astype(v.dtype), v)


def latent_diff_attention(q, k_all, v_all, lam):
    b, n = q.shape[0], q.shape[1]
    nb = n // Q_BLOCK
    qb = jnp.moveaxis(q.reshape(b, nb, Q_BLOCK, A_HEADS, 2, A_QK_DIM), 1, 0)
    o = lax.map(lambda qq: diff_softmax_attend(qq, k_all, v_all, lam), qb)
    return jnp.moveaxis(o, 0, 1).reshape(b, n, A_HEADS, A_V_DIM)


def spatial_gate(z, vnorm_g, vnorm_b, w_spatial, b_spatial):
    b, n = z.shape[0], z.shape[1]
    u, v = z[..., :B_COLS], z[..., B_COLS:]
    v = layer_norm(v.reshape(b, n, B_GROUPS, B_CH),
                   vnorm_g.reshape(B_GROUPS, B_CH), vnorm_b.reshape(B_GROUPS, B_CH))
    v = v.reshape(b, n // CHUNK, CHUNK, B_GROUPS, B_CH)
    sv = jnp.einsum('gpq,bnqgc->bnpgc', w_spatial, v) + b_spatial.T[:, :, None]
    return u * sv.reshape(b, n, B_COLS)


def mixer_ab(h_lat, h_ctx, ctx_out, layer_idx, w_in, w_out, lam_q1, lam_k1, lam_q2, lam_k2,
             subln_g, vnorm_g, vnorm_b, w_spatial, b_spatial):
    b, n, _ = h_lat.shape
    m = h_ctx.shape[1]
    lam_init = 0.8 - 0.6 * math.exp(-0.3 * layer_idx)
    f32 = jnp.float32
    lam = (jnp.exp(jnp.sum(lam_q1.astype(f32) * lam_k1.astype(f32)))
           - jnp.exp(jnp.sum(lam_q2.astype(f32) * lam_k2.astype(f32))) + lam_init)
    i_k, i_v, i_b = A_Q_COLS, 2 * A_Q_COLS, 2 * A_Q_COLS + A_V_COLS
    z = h_lat @ w_in
    rows, cols = grid_positions(n)
    q = axial_rope(z[..., :i_k].reshape(b, n, A_HEADS, 2, A_QK_DIM), rows, cols)
    k = axial_rope(z[..., i_k:i_v].reshape(b, n, A_HEADS, 2, A_QK_DIM), rows, cols)
    v = z[..., i_v:i_b].reshape(b, n, A_HEADS, A_V_DIM)
    if ctx_out:
        zc = h_ctx @ w_in
        zc_kv = zc[..., i_k:i_b]
    else:
        zc_kv = h_ctx @ w_in[:, i_k:i_b]
    kc = zc_kv[..., :A_Q_COLS].reshape(b, m, A_HEADS, 2, A_QK_DIM)
    vc = zc_kv[..., A_Q_COLS:].reshape(b, m, A_HEADS, A_V_DIM)
    k_all = jnp.concatenate([k, kc], axis=1)
    v_all = jnp.concatenate([v, vc], axis=1)
    o = latent_diff_attention(q, k_all, v_all, lam)
    a_lat = (rms_norm(o, subln_g) * (1 - lam_init)).reshape(b, n, A_V_COLS)
    s_lat = spatial_gate(jax.nn.gelu(z[..., i_b:]), vnorm_g, vnorm_b, w_spatial, b_spatial)
    y_lat = jnp.concatenate([a_lat, s_lat], axis=-1) @ w_out
    y_ctx = None
    if ctx_out:
        qc = zc[..., :i_k].reshape(b, m, A_HEADS, 2, A_QK_DIM)
        oc = diff_softmax_attend(qc, kc, vc, lam)
        a_ctx = (rms_norm(oc, subln_g) * (1 - lam_init)).reshape(b, m, A_V_COLS)
        s_ctx = spatial_gate(jax.nn.gelu(zc[..., i_b:]), vnorm_g, vnorm_b, w_spatial, b_spatial)
        y_ctx = jnp.concatenate([a_ctx, s_ctx], axis=-1) @ w_out
    return y_lat, y_ctx


def conformer_conv(z, dw_w, dw_b, norm_g, norm_b):
    y = z[..., :C_CH] * jax.nn.sigmoid(z[..., C_CH:])
    pad = (CONV_W - 1) // 2
    y = lax.conv_general_dilated(y, dw_w[:, None, :], window_strides=(1,), padding=[(pad, pad)],
                                 dimension_numbers=('NWC', 'WIO', 'NWC'),
                                 feature_group_count=C_CH) + dw_b
    return jax.nn.silu(layer_norm(y, norm_g, norm_b))


def fourier_mix(f):
    b, n = f.shape[0], f.shape[1]
    fg = f.reshape(b, n, D_GROUPS, D_CH).astype(jnp.float32)
    out = jnp.real(jnp.fft.fft2(fg, axes=(1, 3), norm='ortho'))
    return out.astype(f.dtype).reshape(b, n, D_COLS)


def mixer_cd(h_lat, h_ctx, w_in, w_out, dw_w, dw_b, norm_g, norm_b):
    def one_sequence(h):
        z = h @ w_in
        yc = conformer_conv(z[..., :2 * C_CH], dw_w, dw_b, norm_g, norm_b)
        yd = fourier_mix(z[..., 2 * C_CH:])
        return jnp.concatenate([yc, yd], axis=-1) @ w_out
    y_lat = one_sequence(h_lat)
    y_ctx = one_sequence(h_ctx) if h_ctx is not None else None
    return y_lat, y_ctx


def swiglu(h, w_gate, w_up, w_down):
    return (jax.nn.silu(h @ w_gate) * (h @ w_up)) @ w_down


def setup_inputs(seed: int = 0) -> dict:
    key = jax.random.key(seed)
    ks = iter(jax.random.split(key, 40))
    nrm = lambda shape, s: jax.random.normal(next(ks), shape, jnp.float32) * s
    D = D_MODEL
    return {
        'x': nrm((BATCH, SEQ, D), 1.0),
        'c': nrm((BATCH, D), 1.0),
        'ctx': nrm((BATCH, CTX_LEN, D), 1.0),
        'c_ctx': nrm((D,), 1.0),
        'mod_w': nrm((DEPTH, D, N_MOD * D), D ** -0.5),
        'mod_b': nrm((DEPTH, N_MOD * D), 0.02),
        'post_mix_g': 1.0 + nrm((DEPTH, D), 0.02),
        'post_ffn_g': 1.0 + nrm((DEPTH, D), 0.02),
        'ffn_w_gate': nrm((DEPTH, D, D_FF), D ** -0.5),
        'ffn_w_up': nrm((DEPTH, D, D_FF), D ** -0.5),
        'ffn_w_down': nrm((DEPTH, D_FF, D), D_FF ** -0.5),
        'ab_w_in': nrm((N_EVEN, D, AB_IN), D ** -0.5),
        'ab_w_out': nrm((N_EVEN, AB_MIX, D), AB_MIX ** -0.5),
        'ab_lam_q1': nrm((N_EVEN, A_QK_DIM), 0.1),
        'ab_lam_k1': nrm((N_EVEN, A_QK_DIM), 0.1),
        'ab_lam_q2': nrm((N_EVEN, A_QK_DIM), 0.1),
        'ab_lam_k2': nrm((N_EVEN, A_QK_DIM), 0.1),
        'ab_subln_g': 1.0 + nrm((N_EVEN, A_V_DIM), 0.02),
        'ab_vnorm_g': 1.0 + nrm((N_EVEN, B_COLS), 0.02),
        'ab_vnorm_b': nrm((N_EVEN, B_COLS), 0.02),
        'ab_w_spatial': nrm((N_EVEN, B_GROUPS, CHUNK, CHUNK), CHUNK ** -0.5),
        'ab_b_spatial': 1.0 + nrm((N_EVEN, B_GROUPS, CHUNK), 0.1),
        'cd_w_in': nrm((N_ODD, D, CD_IN), D ** -0.5),
        'cd_w_out': nrm((N_ODD, CD_MIX, D), CD_MIX ** -0.5),
        'cd_dw_w': nrm((N_ODD, CONV_W, C_CH), CONV_W ** -0.5),
        'cd_dw_b': nrm((N_ODD, C_CH), 0.02),
        'cd_norm_g': 1.0 + nrm((N_ODD, C_CH), 0.02),
        'cd_norm_b': nrm((N_ODD, C_CH), 0.02),
    }


def reference(x, c, ctx, c_ctx, mod_w, mod_b, post_mix_g, post_ffn_g, ffn_w_gate, ffn_w_up,
              ffn_w_down, ab_w_in, ab_w_out, ab_lam_q1, ab_lam_k1, ab_lam_q2, ab_lam_k2,
              ab_subln_g, ab_vnorm_g, ab_vnorm_b, ab_w_spatial, ab_b_spatial, cd_w_in, cd_w_out,
              cd_dw_w, cd_dw_b, cd_norm_g, cd_norm_b):
    x_lat, x_ctx = x, ctx
    silu_c = jax.nn.silu(c)
    silu_cc = jax.nn.silu(c_ctx)
    for l in range(DEPTH):
        last = l == DEPTH - 1
        even = l % 2 == 0
        i = l // 2
        mod = (silu_c @ mod_w[l] + mod_b[l])[:, None, :]
        sh_m, sc_m, g_m, sh_f, sc_f, g_f = jnp.split(mod, N_MOD, axis=-1)
        h_lat = modulate(rms_norm(x_lat), sh_m, sc_m)
        h_ctx = None
        if (not last) or even:
            mod_c = silu_cc @ mod_w[l] + mod_b[l]
            csh_m, csc_m, cg_m, csh_f, csc_f, cg_f = jnp.split(mod_c, N_MOD, axis=-1)
            h_ctx = modulate(rms_norm(x_ctx), csh_m, csc_m)
        if even:
            y_lat, y_ctx = mixer_ab(h_lat, h_ctx, not last, l, ab_w_in[i], ab_w_out[i],
                                    ab_lam_q1[i], ab_lam_k1[i], ab_lam_q2[i], ab_lam_k2[i],
                                    ab_subln_g[i], ab_vnorm_g[i], ab_vnorm_b[i],
                                    ab_w_spatial[i], ab_b_spatial[i])
        else:
            y_lat, y_ctx = mixer_cd(h_lat, h_ctx, cd_w_in[i], cd_w_out[i], cd_dw_w[i],
                                    cd_dw_b[i], cd_norm_g[i], cd_norm_b[i])
        x_lat = x_lat + g_m * rms_norm(y_lat, post_mix_g[l])
        f_lat = swiglu(modulate(rms_norm(x_lat), sh_f, sc_f), ffn_w_gate[l], ffn_w_up[l], ffn_w_down[l])
        x_lat = x_lat + g_f * rms_norm(f_lat, post_ffn_g[l])
        if not last:
            x_ctx = x_ctx + cg_m * rms_norm(y_ctx, post_mix_g[l])
            f_ctx = swiglu(modulate(rms_norm(x_ctx), csh_f, csc_f), ffn_w_gate[l], ffn_w_up[l], ffn_w_down[l])
            x_ctx = x_ctx + cg_f * rms_norm(f_ctx, post_ffn_g[l])
    return x_lat
```

```python
import functools
import math

import numpy as np
import jax
import jax.numpy as jnp
from jax import lax
from jax.experimental import pallas as pl
from jax.experimental.pallas import tpu as pltpu

F32 = jnp.float32
BF16 = jnp.bfloat16

D_MODEL = 2048
SEQ = 4096
GRID_W = 64
CTX_LEN = 256
EPS = 1e-6
N_MOD = 6
D_FF = 5632
HEADS = 8
QK_DIM = 64
HEAD_W = 128
SEC = 1024
CHUNK = 128
CONV_W = 31
CONV_PAD = (CONV_W - 1) // 2
ROPE_HALF = 32
ROPE_FREQS = 16
ROPE_BASE = 10000.0
FFT_R = 64
N_COND = 8
CTX_ROW = 4

LANE = 128
VMEM_LIMIT = 56 * 1024 * 1024
TM = 512
TF = 512
TQ = 256
CK = 256
CONV_T = 512
CONV_RB = 64
HALO = 16


def _cparams(*sem):
    return pltpu.CompilerParams(dimension_semantics=sem, vmem_limit_bytes=VMEM_LIMIT)


def _resident():
    return pl.BlockSpec(memory_space=pltpu.VMEM)


def _mod_spec(layer, which, rowfn):
    return pl.BlockSpec((1, 1, D_MODEL),
                        lambda i, *_: ((layer * N_COND + rowfn(i)) * N_MOD + which, 0, 0))


def _lat_row(tm):
    return lambda i: i // (SEQ // tm)


def _ctx_row(_tm):
    return lambda i: CTX_ROW


def _rms(x):
    return x * lax.rsqrt(jnp.mean(x * x, axis=-1, keepdims=True) + EPS)


def _mod_kernel(c_ref, w_ref, b_ref, o_ref):
    s = jax.nn.silu(c_ref[...]).astype(BF16)
    o_ref[0] = jnp.dot(s, w_ref[0].astype(BF16), preferred_element_type=F32) + b_ref[0]


def _modulation(cond, mod_w, mod_b):
    depth, _, n = mod_w.shape
    tn = 1024
    return pl.pallas_call(
        _mod_kernel,
        grid=(depth, n // tn),
        in_specs=[pl.BlockSpec((N_COND, D_MODEL), lambda l, j: (0, 0)),
                  pl.BlockSpec((1, D_MODEL, tn), lambda l, j: (l, 0, j)),
                  pl.BlockSpec((1, 1, tn), lambda l, j: (l, 0, j))],
        out_specs=pl.BlockSpec((1, N_COND, tn), lambda l, j: (l, 0, j)),
        out_shape=jax.ShapeDtypeStruct((depth, N_COND, n), F32),
        compiler_params=_cparams("parallel", "parallel"),
        name="adaln_mod",
    )(cond, mod_w, mod_b.reshape(depth, 1, n))


def _inproj_ab_kernel(*refs, rope):
    if rope:
        (x_ref, sh_ref, sc_ref, w_ref, vg_ref, vb_ref, cos_ref, s1_ref, s2_ref,
         q_ref, k_ref, v_ref, u_ref, vn_ref) = refs
    else:
        x_ref, sh_ref, sc_ref, w_ref, vg_ref, vb_ref, q_ref, k_ref, v_ref, u_ref, vn_ref = refs
    h = _rms(x_ref[...]) * (1.0 + sc_ref[0]) + sh_ref[0]
    hb = h.astype(BF16)

    def section(s):
        return jnp.dot(hb, w_ref[:, s * SEC:(s + 1) * SEC], preferred_element_type=F32)

    def rotate(z):
        if not rope:
            return z
        cos, s1, s2 = cos_ref[...], s1_ref[...], s2_ref[...]
        outs = []
        for hd in range(HEADS):
            zs = z[:, hd * HEAD_W:(hd + 1) * HEAD_W]
            outs.append(zs * cos + pltpu.roll(zs, HEAD_W - ROPE_FREQS, 1) * s1
                        + pltpu.roll(zs, ROPE_FREQS, 1) * s2)
        return jnp.concatenate(outs, axis=1)

    q_ref[...] = (rotate(section(0)) * (QK_DIM ** -0.5)).astype(BF16)
    k_ref[...] = rotate(section(1)).astype(BF16)
    v_ref[...] = section(2).astype(BF16)
    u_ref[...] = jax.nn.gelu(section(3)).astype(BF16)
    g = jax.nn.gelu(section(4))
    outs = []
    for gi in range(SEC // LANE):
        sl = slice(gi * LANE, (gi + 1) * LANE)
        gs = g[:, sl]
        d = gs - jnp.mean(gs, axis=-1, keepdims=True)
        var = jnp.mean(d * d, axis=-1, keepdims=True)
        outs.append(d * lax.rsqrt(var + EPS) * vg_ref[:, sl] + vb_ref[:, sl])
    vn_ref[...] = jnp.concatenate(outs, axis=1).astype(BF16)


def _inproj_ab(x2d, mod3, layer, rowfn_maker, w_in, vnorm_g, vnorm_b, rope_tabs):
    rows = x2d.shape[0]
    tm = TM
    rowfn = rowfn_maker(tm)
    rope = rope_tabs is not None
    row_spec = pl.BlockSpec((tm, D_MODEL), lambda i: (i, 0))
    sec_spec = pl.BlockSpec((tm, SEC), lambda i: (i, 0))
    vec_spec = pl.BlockSpec((1, SEC), lambda i: (0, 0))
    in_specs = [row_spec, _mod_spec(layer, 0, rowfn), _mod_spec(layer, 1, rowfn), _resident(),
                vec_spec, vec_spec]
    args = [x2d, mod3, mod3, w_in, vnorm_g.reshape(1, SEC), vnorm_b.reshape(1, SEC)]
    if rope:
        tab_spec = pl.BlockSpec((tm, HEAD_W), lambda i: (i % (SEQ // tm), 0))
        in_specs += [tab_spec] * 3
        args += list(rope_tabs)
    out = jax.ShapeDtypeStruct((rows, SEC), BF16)
    return pl.pallas_call(
        functools.partial(_inproj_ab_kernel, rope=rope),
        grid=(rows // tm,),
        in_specs=in_specs,
        out_specs=[sec_spec] * 5,
        out_shape=[out] * 5,
        compiler_params=_cparams("parallel"),
        name="inproj_ab_lat" if rope else "inproj_ab_ctx",
    )(*args)


def _attn_kernel(*refs, seg_lens, tq, lam_init):
    nseg = len(seg_lens)
    q_ref = refs[0]
    k_refs = refs[1:1 + nseg]
    v_refs = refs[1 + nseg:1 + 2 * nseg]
    lam_ref, g_ref, o_ref = refs[1 + 2 * nseg:]
    q = q_ref[0]
    lane = lax.broadcasted_iota(jnp.int32, (tq, HEAD_W), 1)
    zero = jnp.zeros_like(q)
    qq = jnp.concatenate([jnp.where(lane < QK_DIM, q, zero), jnp.where(lane >= QK_DIM, q, zero)],
                         axis=0)
    m = jnp.full((2 * tq, 1), -jnp.inf, F32)
    l = jnp.zeros((2 * tq, 1), F32)
    acc = jnp.zeros((2 * tq, HEAD_W), F32)

    for k_ref, v_ref, n_keys in zip(k_refs, v_refs, seg_lens):
        def step(c, carry, k_ref=k_ref, v_ref=v_ref):
            m, l, acc = carry
            c0 = pl.multiple_of(c * CK, CK)
            kc = k_ref[0, pl.ds(c0, CK), :]
            vc = v_ref[0, pl.ds(c0, CK), :]
            s = lax.dot_general(qq, kc, (((1,), (1,)), ((), ())), preferred_element_type=F32)
            m_new = jnp.maximum(m, jnp.max(s, axis=-1, keepdims=True))
            alpha = jnp.exp(m - m_new)
            p = jnp.exp(s - m_new)
            l = alpha * l + jnp.sum(p, axis=-1, keepdims=True)
            acc = alpha * acc + jnp.dot(p.astype(BF16), vc, preferred_element_type=F32)
            return m_new, l, acc

        m, l, acc = lax.fori_loop(0, n_keys // CK, step, (m, l, acc))

    lam_t = lam_ref[...]
    e1 = jnp.exp(jnp.sum(lam_t[0:1, :] * lam_t[1:2, :], axis=-1, keepdims=True))
    e2 = jnp.exp(jnp.sum(lam_t[2:3, :] * lam_t[3:4, :], axis=-1, keepdims=True))
    lam = e1 - e2 + lam_init
    o = acc[:tq] / l[:tq] - lam * (acc[tq:] / l[tq:])
    o = _rms(o) * g_ref[...] * (1.0 - lam_init)
    o_ref[0] = o.astype(BF16)


def _diff_attention(q, ks, vs, lam_tab, subln_g, lam_init):
    batch, nq, _ = q.shape
    tq = min(TQ, nq)
    seg_lens = tuple(int(k.shape[1]) for k in ks)
    q_spec = pl.BlockSpec((1, tq, HEAD_W), lambda b, h, i: (b, i, h))
    kv_specs = [pl.BlockSpec((1, n, HEAD_W), lambda b, h, i: (b, 0, h)) for n in seg_lens]
    return pl.pallas_call(
        functools.partial(_attn_kernel, seg_lens=seg_lens, tq=tq, lam_init=lam_init),
        grid=(batch, HEADS, nq // tq),
        in_specs=[q_spec] + kv_specs + kv_specs
                 + [pl.BlockSpec((N_COND, LANE), lambda b, h, i: (0, 0)),
                    pl.BlockSpec((1, HEAD_W), lambda b, h, i: (0, 0))],
        out_specs=q_spec,
        out_shape=jax.ShapeDtypeStruct(q.shape, BF16),
        compiler_params=_cparams("parallel", "parallel", "arbitrary"),
        name="diff_attn_%d" % nq,
    )(q, *ks, *vs, lam_tab, subln_g.reshape(1, HEAD_W))


def _mix_epilogue(y, x_ref, gate_ref, pg_ref, o_ref):
    o_ref[...] = x_ref[...] + gate_ref[0] * (_rms(y) * pg_ref[...])


def _outproj_ab_kernel(a_ref, u_ref, vn_ref, wsp_ref, bsp_ref, wo_ref, x_ref, gate_ref, pg_ref,
                       o_ref, s_scr, *, tm):
    for c in range(tm // CHUNK):
        rs = slice(c * CHUNK, (c + 1) * CHUNK)
        for g in range(SEC // LANE):
            cs = slice(g * LANE, (g + 1) * LANE)
            sv = jnp.dot(wsp_ref[g], vn_ref[rs, cs], preferred_element_type=F32) + bsp_ref[g]
            s_scr[rs, cs] = (u_ref[rs, cs].astype(F32) * sv).astype(BF16)
    y = (jnp.dot(a_ref[...], wo_ref[0:SEC, :], preferred_element_type=F32)
         + jnp.dot(s_scr[...], wo_ref[SEC:2 * SEC, :], preferred_element_type=F32))
    _mix_epilogue(y, x_ref, gate_ref, pg_ref, o_ref)


def _outproj_ab(a2d, u2d, vn2d, w_sp, b_sp, w_out, x2d, mod3, layer, rowfn_maker, post_g):
    rows = x2d.shape[0]
    tm = TM
    rowfn = rowfn_maker(tm)
    sec_spec = pl.BlockSpec((tm, SEC), lambda i: (i, 0))
    row_spec = pl.BlockSpec((tm, D_MODEL), lambda i: (i, 0))
    return pl.pallas_call(
        functools.partial(_outproj_ab_kernel, tm=tm),
        grid=(rows // tm,),
        in_specs=[sec_spec, sec_spec, sec_spec, _resident(), _resident(), _resident(), row_spec,
                  _mod_spec(layer, 2, rowfn), pl.BlockSpec((1, D_MODEL), lambda i: (0, 0))],
        out_specs=row_spec,
        out_shape=jax.ShapeDtypeStruct((rows, D_MODEL), F32),
        scratch_shapes=[pltpu.VMEM((tm, SEC), BF16)],
        compiler_params=_cparams("parallel"),
        name="outproj_ab_%d" % rows,
    )(a2d, u2d, vn2d, w_sp, b_sp, w_out, x2d, mod3, post_g.reshape(1, D_MODEL))


def _outproj_cd_kernel(yc_ref, yd_ref, wo_ref, x_ref, gate_ref, pg_ref, o_ref):
    y = (jnp.dot(yc_ref[...], wo_ref[0:SEC, :], preferred_element_type=F32)
         + jnp.dot(yd_ref[...], wo_ref[SEC:2 * SEC, :], preferred_element_type=F32))
    _mix_epilogue(y, x_ref, gate_ref, pg_ref, o_ref)


def _outproj_cd(yc2d, yd2d, w_out, x2d, mod3, layer, rowfn_maker, post_g):
    rows = x2d.shape[0]
    tm = TM
    rowfn = rowfn_maker(tm)
    sec_spec = pl.BlockSpec((tm, SEC), lambda i: (i, 0))
    row_spec = pl.BlockSpec((tm, D_MODEL), lambda i: (i, 0))
    return pl.pallas_call(
        _outproj_cd_kernel,
        grid=(rows // tm,),
        in_specs=[sec_spec, sec_spec, _resident(), row_spec, _mod_spec(layer, 2, rowfn),
                  pl.BlockSpec((1, D_MODEL), lambda i: (0, 0))],
        out_specs=row_spec,
        out_shape=jax.ShapeDtypeStruct((rows, D_MODEL), F32),
        compiler_params=_cparams("parallel"),
        name="outproj_cd",
    )(yc2d, yd2d, w_out, x2d, mod3, post_g.reshape(1, D_MODEL))


def _ffn_kernel(x_ref, sh_ref, sc_ref, gate_ref, pg_ref, wg_ref, wu_ref, wd_ref, o_ref,
                hb_scr, acc_scr):
    j = pl.program_id(1)

    @pl.when(j == 0)
    def _():
        h = _rms(x_ref[...]) * (1.0 + sc_ref[0]) + sh_ref[0]
        hb_scr[...] = h.astype(BF16)
        acc_scr[...] = jnp.zeros_like(acc_scr)

    hb = hb_scr[...]
    g = jnp.dot(hb, wg_ref[...], preferred_element_type=F32)
    u = jnp.dot(hb, wu_ref[...], preferred_element_type=F32)
    a = (jax.nn.silu(g) * u).astype(BF16)
    acc_scr[...] += jnp.dot(a, wd_ref[...], preferred_element_type=F32)

    @pl.when(j == pl.num_programs(1) - 1)
    def _():
        _mix_epilogue(acc_scr[...], x_ref, gate_ref, pg_ref, o_ref)


def _ffn(x2d, mod3, layer, rowfn_maker, post_g, w_gate, w_up, w_down):
    rows = x2d.shape[0]
    tm = TM
    rowfn = rowfn_maker(tm)
    row_spec = pl.BlockSpec((tm, D_MODEL), lambda i, j: (i, 0))
    return pl.pallas_call(
        _ffn_kernel,
        grid=(rows // tm, D_FF // TF),
        in_specs=[row_spec, _mod_spec(layer, 3, rowfn), _mod_spec(layer, 4, rowfn),
                  _mod_spec(layer, 5, rowfn), pl.BlockSpec((1, D_MODEL), lambda i, j: (0, 0)),
                  pl.BlockSpec((D_MODEL, TF), lambda i, j: (0, j)),
                  pl.BlockSpec((D_MODEL, TF), lambda i, j: (0, j)),
                  pl.BlockSpec((TF, D_MODEL), lambda i, j: (j, 0))],
        out_specs=row_spec,
        out_shape=jax.ShapeDtypeStruct((rows, D_MODEL), F32),
        scratch_shapes=[pltpu.VMEM((tm, D_MODEL), BF16), pltpu.VMEM((tm, D_MODEL), F32)],
        compiler_params=_cparams("parallel", "arbitrary"),
        name="ffn_%d" % rows,
    )(x2d, mod3, mod3, mod3, post_g.reshape(1, D_MODEL), w_gate, w_up, w_down)


def _inproj_cd_kernel(x_ref, sh_ref, sc_ref, w_ref, y_ref, f_ref):
    h = _rms(x_ref[...]) * (1.0 + sc_ref[0]) + sh_ref[0]
    hb = h.astype(BF16)

    def section(s):
        return jnp.dot(hb, w_ref[:, s * SEC:(s + 1) * SEC], preferred_element_type=F32)

    y_ref[...] = section(0) * jax.nn.sigmoid(section(1))
    f_ref[...] = section(2).astype(BF16)


def _inproj_cd(x2d, mod3, layer, rowfn_maker, w_in):
    rows = x2d.shape[0]
    tm = TM
    rowfn = rowfn_maker(tm)
    sec_spec = pl.BlockSpec((tm, SEC), lambda i: (i, 0))
    return pl.pallas_call(
        _inproj_cd_kernel,
        grid=(rows // tm,),
        in_specs=[pl.BlockSpec((tm, D_MODEL), lambda i: (i, 0)), _mod_spec(layer, 0, rowfn),
                  _mod_spec(layer, 1, rowfn), _resident()],
        out_specs=[sec_spec, sec_spec],
        out_shape=[jax.ShapeDtypeStruct((rows, SEC), F32), jax.ShapeDtypeStruct((rows, SEC), BF16)],
        compiler_params=_cparams("parallel"),
        name="inproj_cd",
    )(x2d, mod3, mod3, w_in)


def _conv_kernel(y_ref, prev_ref, next_ref, w_ref, b_ref, g_ref, be_ref, o_ref, ybuf, cbuf):
    i = pl.program_id(1)
    t = y_ref.shape[1]
    zeros = jnp.zeros((HALO, SEC), F32)
    ybuf[0:HALO, :] = jnp.where(i > 0, prev_ref[0], zeros)
    ybuf[HALO:HALO + t, :] = y_ref[0]
    ybuf[HALO + t:HALO + t + HALO, :] = jnp.where(i < pl.num_programs(1) - 1, next_ref[0], zeros)
    off = HALO - CONV_PAD

    def conv_block(rb, carry):
        r0 = pl.multiple_of(rb * CONV_RB, CONV_RB)
        for c in range(SEC // LANE):
            cs = slice(c * LANE, (c + 1) * LANE)
            win = ybuf[pl.ds(r0, CONV_RB + 2 * HALO), cs]
            acc = jnp.zeros((CONV_RB, LANE), F32) + b_ref[:, cs]
            for w in range(CONV_W):
                acc = acc + win[off + w:off + w + CONV_RB, :] * w_ref[w:w + 1, cs]
            cbuf[pl.ds(r0, CONV_RB), cs] = acc
        return carry

    lax.fori_loop(0, t // CONV_RB, conv_block, 0)

    def norm_block(rb, carry):
        r0 = pl.multiple_of(rb * CONV_RB, CONV_RB)
        v = cbuf[pl.ds(r0, CONV_RB), :]
        d = v - jnp.mean(v, axis=-1, keepdims=True)
        var = jnp.mean(d * d, axis=-1, keepdims=True)
        z = d * lax.rsqrt(var + EPS) * g_ref[...] + be_ref[...]
        o_ref[0, pl.ds(r0, CONV_RB), :] = jax.nn.silu(z).astype(BF16)
        return carry

    lax.fori_loop(0, t // CONV_RB, norm_block, 0)


def _conformer_conv(y3d, dw_w, dw_b, norm_g, norm_b):
    batch, n, _ = y3d.shape
    t = CONV_T
    hb = t // HALO
    nh = n // HALO
    vec = pl.BlockSpec((1, SEC), lambda b, i: (0, 0))
    return pl.pallas_call(
        _conv_kernel,
        grid=(batch, n // t),
        in_specs=[pl.BlockSpec((1, t, SEC), lambda b, i: (b, i, 0)),
                  pl.BlockSpec((1, HALO, SEC), lambda b, i: (b, jnp.maximum(i * hb - 1, 0), 0)),
                  pl.BlockSpec((1, HALO, SEC),
                               lambda b, i: (b, jnp.minimum((i + 1) * hb, nh - 1), 0)),
                  pl.BlockSpec((CONV_W, SEC), lambda b, i: (0, 0)), vec, vec, vec],
        out_specs=pl.BlockSpec((1, t, SEC), lambda b, i: (b, i, 0)),
        out_shape=jax.ShapeDtypeStruct((batch, n, SEC), BF16),
        scratch_shapes=[pltpu.VMEM((t + 2 * HALO, SEC), F32), pltpu.VMEM((t, SEC), F32)],
        compiler_params=_cparams("parallel", "arbitrary"),
        name="conformer_conv",
    )(y3d, y3d, y3d, dw_w, dw_b.reshape(1, SEC), norm_g.reshape(1, SEC), norm_b.reshape(1, SEC))


def _dft_constants():
    c = np.arange(LANE)
    ang_c = 2.0 * np.pi * np.outer(c, c) / LANE
    norm = 1.0 / math.sqrt(SEQ * LANE)
    chan = np.concatenate([np.cos(ang_c), -np.sin(ang_c)], axis=1) * norm
    r = np.arange(FFT_R)
    ang_r = 2.0 * np.pi * np.outer(r, r) / FFT_R
    cr, ci = np.cos(ang_r), -np.sin(ang_r)
    stage1 = np.block([[cr, -ci], [ci, cr]])
    stage2 = np.concatenate([cr, -ci], axis=1)
    ang_t = 2.0 * np.pi * np.outer(r, r) / SEQ
    return (jnp.asarray(chan, F32).astype(BF16), jnp.asarray(stage1, F32).astype(BF16),
            jnp.asarray(stage2, F32).astype(BF16),
            jnp.asarray(np.cos(ang_t), F32), jnp.asarray(-np.sin(ang_t), F32))


def _fft_a_kernel(x_ref, chan_ref, m1_ref, twc_ref, tws_ref, ar_ref, ai_ref):
    x = x_ref[0]
    nslab = x.shape[1] // LANE
    xs = jnp.concatenate([x[:, s * LANE:(s + 1) * LANE] for s in range(nslab)], axis=0)
    z = jnp.dot(xs, chan_ref[...], preferred_element_type=F32).astype(BF16)
    zr = jnp.concatenate([z[s * FFT_R:(s + 1) * FFT_R, 0:LANE] for s in range(nslab)], axis=1)
    zi = jnp.concatenate([z[s * FFT_R:(s + 1) * FFT_R, LANE:2 * LANE] for s in range(nslab)],
                         axis=1)
    a = jnp.dot(m1_ref[...], jnp.concatenate([zr, zi], axis=0), preferred_element_type=F32)
    a_r, a_i = a[0:FFT_R], a[FFT_R:2 * FFT_R]
    for j in range(x.shape[1] // SEC):
        cs = slice(j * SEC, (j + 1) * SEC)
        tc = jnp.concatenate([twc_ref[j]] * (SEC // LANE), axis=1)
        ts = jnp.concatenate([tws_ref[j]] * (SEC // LANE), axis=1)
        ar_ref[0, j] = (a_r[:, cs] * tc - a_i[:, cs] * ts).astype(BF16)
        ai_ref[0, j] = (a_r[:, cs] * ts + a_i[:, cs] * tc).astype(BF16)


def _fft_c_kernel(ar_ref, ai_ref, m2_ref, o_ref):
    a = jnp.concatenate([ar_ref[0], ai_ref[0]], axis=0)
    o_ref[0] = jnp.dot(m2_ref[...], a, preferred_element_type=F32).astype(BF16)


def _fourier_mix(f3d):
    batch, n, _ = f3d.shape
    chan, m1, m2, twc, tws = _dft_constants()
    twc = jnp.broadcast_to(twc[:, :, None], (FFT_R, FFT_R, LANE))
    tws = jnp.broadcast_to(tws[:, :, None], (FFT_R, FFT_R, LANE))
    nb = 8
    wide = nb * SEC
    xv = f3d.reshape(batch, FFT_R, FFT_R * SEC)
    blk = pl.BlockSpec((1, FFT_R, wide), lambda b, j: (b, 0, j))
    a_spec = pl.BlockSpec((1, nb, FFT_R, SEC), lambda b, j: (b, j, 0, 0))
    tw_spec = pl.BlockSpec((nb, FFT_R, LANE), lambda b, j: (j, 0, 0))
    a_shape = jax.ShapeDtypeStruct((batch, FFT_R, FFT_R, SEC), BF16)
    a_r, a_i = pl.pallas_call(
        _fft_a_kernel,
        grid=(batch, FFT_R // nb),
        in_specs=[blk, pl.BlockSpec((LANE, 2 * LANE), lambda b, j: (0, 0)),
                  pl.BlockSpec((2 * FFT_R, 2 * FFT_R), lambda b, j: (0, 0)), tw_spec, tw_spec],
        out_specs=[a_spec, a_spec],
        out_shape=[a_shape, a_shape],
        compiler_params=_cparams("parallel", "parallel"),
        name="fourier_stage_a",
    )(xv, chan, m1, twc, tws)
    out = pl.pallas_call(
        _fft_c_kernel,
        grid=(batch, FFT_R // nb),
        in_specs=[blk, blk, pl.BlockSpec((FFT_R, 2 * FFT_R), lambda b, j: (0, 0))],
        out_specs=blk,
        out_shape=jax.ShapeDtypeStruct((batch, FFT_R, FFT_R * SEC), BF16),
        compiler_params=_cparams("parallel", "parallel"),
        name="fourier_stage_c",
    )(a_r.reshape(batch, FFT_R, FFT_R * SEC), a_i.reshape(batch, FFT_R, FFT_R * SEC), m2)
    return out.reshape(batch, n, SEC)


def _rope_tables(n):
    rows = jnp.repeat(jnp.arange(n // GRID_W, dtype=F32), GRID_W)
    cols = jnp.tile(jnp.arange(GRID_W, dtype=F32), n // GRID_W)
    inv = ROPE_BASE ** (-jnp.arange(ROPE_FREQS, dtype=F32) / ROPE_FREQS)
    lane = np.arange(HEAD_W)
    freq = lane % ROPE_FREQS
    by_col = (lane % QK_DIM) // ROPE_HALF == 1
    second = (lane % ROPE_HALF) // ROPE_FREQS == 1
    ang = jnp.where(by_col[None, :], (cols[:, None] * inv)[:, freq], (rows[:, None] * inv)[:, freq])
    cos, sin = jnp.cos(ang), jnp.sin(ang)
    zero = jnp.zeros_like(sin)
    return cos, jnp.where(second[None, :], zero, -sin), jnp.where(second[None, :], sin, zero)


def kernel(x, c, ctx, c_ctx, mod_w, mod_b, post_mix_g, post_ffn_g, ffn_w_gate, ffn_w_up, ffn_w_down, ab_w_in, ab_w_out, ab_lam_q1, ab_lam_k1, ab_lam_q2, ab_lam_k2, ab_subln_g, ab_vnorm_g, ab_vnorm_b, ab_w_spatial, ab_b_spatial, cd_w_in, cd_w_out, cd_dw_w, cd_dw_b, cd_norm_g, cd_norm_b):
    batch, n, d = x.shape
    m = ctx.shape[1]
    depth = mod_w.shape[0]
    assert (n, d, m) == (SEQ, D_MODEL, CTX_LEN) and batch <= CTX_ROW

    cond = jnp.concatenate([c, c_ctx[None, :], jnp.zeros((N_COND - batch - 1, d), F32)], axis=0)
    mod3 = _modulation(cond, mod_w, mod_b).reshape(depth * N_COND * N_MOD, 1, d)
    rope_tabs = _rope_tables(n)

    x_lat = x.reshape(batch * n, d)
    x_ctx = ctx.reshape(batch * m, d)
    for l in range(depth):
        last = l == depth - 1
        even = l % 2 == 0
        i = l // 2
        use_ctx = (not last) or even
        if even:
            lam_init = 0.8 - 0.6 * math.exp(-0.3 * l)
            w_in = ab_w_in[i].astype(BF16)
            w_out = ab_w_out[i].astype(BF16)
            w_sp = ab_w_spatial[i].astype(BF16)
            b_sp = ab_b_spatial[i].reshape(SEC // LANE, CHUNK, 1)
            lam_tab = jnp.zeros((N_COND, LANE), F32).at[0:4, 0:QK_DIM].set(
                jnp.stack([ab_lam_q1[i], ab_lam_k1[i], ab_lam_q2[i], ab_lam_k2[i]]))
            q, k, v, u, vn = _inproj_ab(x_lat, mod3, l, _lat_row, w_in, ab_vnorm_g[i],
                                        ab_vnorm_b[i], rope_tabs)
            qc, kc, vc, uc, vnc = _inproj_ab(x_ctx, mod3, l, _ctx_row, w_in, ab_vnorm_g[i],
                                             ab_vnorm_b[i], None)
            q3, k3, v3 = (t.reshape(batch, n, SEC) for t in (q, k, v))
            kc3, vc3 = kc.reshape(batch, m, SEC), vc.reshape(batch, m, SEC)
            a_lat = _diff_attention(q3, [k3, kc3], [v3, vc3], lam_tab, ab_subln_g[i], lam_init)
            x_lat_mix = _outproj_ab(a_lat.reshape(batch * n, SEC), u, vn, w_sp, b_sp, w_out, x_lat,
                                    mod3, l, _lat_row, post_mix_g[l])
            if not last:
                a_ctx = _diff_attention(qc.reshape(batch, m, SEC), [kc3], [vc3], lam_tab,
                                        ab_subln_g[i], lam_init)
                x_ctx_mix = _outproj_ab(a_ctx.reshape(batch * m, SEC), uc, vnc, w_sp, b_sp, w_out,
                                        x_ctx, mod3, l, _ctx_row, post_mix_g[l])
        else:
            w_in = cd_w_in[i].astype(BF16)
            w_out = cd_w_out[i].astype(BF16)

            def mix_cd(x2d, rows_per_seq, rowfn_maker):
                y, f = _inproj_cd(x2d, mod3, l, rowfn_maker, w_in)
                nb = x2d.shape[0] // rows_per_seq
                yc = _conformer_conv(y.reshape(nb, rows_per_seq, SEC), cd_dw_w[i], cd_dw_b[i],
                                     cd_norm_g[i], cd_norm_b[i])
                yd = _fourier_mix(f.reshape(nb, rows_per_seq, SEC))
                return _outproj_cd(yc.reshape(-1, SEC), yd.reshape(-1, SEC), w_out, x2d, mod3, l,
                                   rowfn_maker, post_mix_g[l])

            x_lat_mix = mix_cd(x_lat, n, _lat_row)
            if use_ctx:
                raise NotImplementedError("odd non-final layers are outside this problem's depth")
        wg, wu, wd = (ffn_w_gate[l].astype(BF16), ffn_w_up[l].astype(BF16),
                      ffn_w_down[l].astype(BF16))
        x_lat = _ffn(x_lat_mix, mod3, l, _lat_row, post_ffn_g[l], wg, wu, wd)
        if not last:
            x_ctx = _ffn(x_ctx_mix, mod3, l, _ctx_row, post_ffn_g[l], wg, wu, wd)
    return x_lat.reshape(batch, n, d)
```

```python
import functools
import math

import numpy as np
import jax
import jax.numpy as jnp
from jax import lax
from jax.experimental import pallas as pl
from jax.experimental.pallas import tpu as pltpu

F32 = jnp.float32
BF16 = jnp.bfloat16

D_MODEL = 2048
SEQ = 4096
GRID_W = 64
CTX_LEN = 256
EPS = 1e-6
N_MOD = 6
D_FF = 5632
HEADS = 8
QK_DIM = 64
HEAD_W = 128
SEC = 1024
CHUNK = 128
CONV_W = 31
CONV_PAD = (CONV_W - 1) // 2
ROPE_HALF = 32
ROPE_FREQS = 16
ROPE_BASE = 10000.0
FFT_R = 64
N_COND = 8
CTX_ROW = 4

LANE = 128
SUBLANE = 8
VMEM_LIMIT = 56 * 1024 * 1024
TM = 512
TF = 512
TQ = 256
CK = 512
CONV_T = 512
CONV_RB = 64
HALO = 16
FFT_NB = 8


def _cparams(*sem):
    return pltpu.CompilerParams(dimension_semantics=sem, vmem_limit_bytes=VMEM_LIMIT)


def _resident():
    return pl.BlockSpec(memory_space=pltpu.VMEM)


def _mod_spec(layer, which, rowfn):
    return pl.BlockSpec((1, 1, D_MODEL),
                        lambda i, *_: ((layer * N_COND + rowfn(i)) * N_MOD + which, 0, 0))


def _lat_row(tm):
    return lambda i: i // (SEQ // tm)


def _ctx_row(_tm):
    return lambda i: CTX_ROW


def _rms(x):
    return x * lax.rsqrt(jnp.mean(x * x, axis=-1, keepdims=True) + EPS)


def _mod_kernel(c_ref, w_ref, b_ref, o_ref):
    s = jax.nn.silu(c_ref[...]).astype(BF16)
    o_ref[0] = jnp.dot(s, w_ref[0].astype(BF16), preferred_element_type=F32) + b_ref[0]


def _modulation(cond, mod_w, mod_b):
    depth, _, n = mod_w.shape
    tn = 1024
    return pl.pallas_call(
        _mod_kernel,
        grid=(depth, n // tn),
        in_specs=[pl.BlockSpec((N_COND, D_MODEL), lambda l, j: (0, 0)),
                  pl.BlockSpec((1, D_MODEL, tn), lambda l, j: (l, 0, j)),
                  pl.BlockSpec((1, 1, tn), lambda l, j: (l, 0, j))],
        out_specs=pl.BlockSpec((1, N_COND, tn), lambda l, j: (l, 0, j)),
        out_shape=jax.ShapeDtypeStruct((depth, N_COND, n), F32),
        compiler_params=_cparams("parallel", "parallel"),
        name="adaln_mod",
    )(cond, mod_w, mod_b.reshape(depth, 1, n))


def _inproj_ab_kernel(*refs, rope):
    if rope:
        (x_ref, sh_ref, sc_ref, w_ref, vg_ref, vb_ref, cos_ref, s1_ref, s2_ref,
         q_ref, k_ref, v_ref, u_ref, vn_ref) = refs
    else:
        x_ref, sh_ref, sc_ref, w_ref, vg_ref, vb_ref, q_ref, k_ref, v_ref, u_ref, vn_ref = refs
    h = _rms(x_ref[...]) * (1.0 + sc_ref[0]) + sh_ref[0]
    hb = h.astype(BF16)

    def section(s):
        return jnp.dot(hb, w_ref[:, s * SEC:(s + 1) * SEC], preferred_element_type=F32)

    def rotate(z):
        if not rope:
            return z
        cos, s1, s2 = cos_ref[...], s1_ref[...], s2_ref[...]
        outs = []
        for hd in range(HEADS):
            zs = z[:, hd * HEAD_W:(hd + 1) * HEAD_W]
            outs.append(zs * cos + pltpu.roll(zs, HEAD_W - ROPE_FREQS, 1) * s1
                        + pltpu.roll(zs, ROPE_FREQS, 1) * s2)
        return jnp.concatenate(outs, axis=1)

    q_ref[...] = (rotate(section(0)) * (QK_DIM ** -0.5)).astype(BF16)
    k_ref[...] = rotate(section(1)).astype(BF16)
    v_ref[...] = section(2).astype(BF16)
    u_ref[...] = jax.nn.gelu(section(3)).astype(BF16)
    g = jax.nn.gelu(section(4))
    outs = []
    for gi in range(SEC // LANE):
        sl = slice(gi * LANE, (gi + 1) * LANE)
        gs = g[:, sl]
        d = gs - jnp.mean(gs, axis=-1, keepdims=True)
        var = jnp.mean(d * d, axis=-1, keepdims=True)
        outs.append(d * lax.rsqrt(var + EPS) * vg_ref[:, sl] + vb_ref[:, sl])
    vn_ref[...] = jnp.concatenate(outs, axis=1).astype(BF16)


def _inproj_ab(x2d, mod3, layer, rowfn_maker, w_in, vnorm_g, vnorm_b, rope_tabs):
    rows = x2d.shape[0]
    tm = TM
    rowfn = rowfn_maker(tm)
    rope = rope_tabs is not None
    row_spec = pl.BlockSpec((tm, D_MODEL), lambda i: (i, 0))
    sec_spec = pl.BlockSpec((tm, SEC), lambda i: (i, 0))
    vec_spec = pl.BlockSpec((1, SEC), lambda i: (0, 0))
    in_specs = [row_spec, _mod_spec(layer, 0, rowfn), _mod_spec(layer, 1, rowfn), _resident(),
                vec_spec, vec_spec]
    args = [x2d, mod3, mod3, w_in, vnorm_g.reshape(1, SEC), vnorm_b.reshape(1, SEC)]
    if rope:
        tab_spec = pl.BlockSpec((tm, HEAD_W), lambda i: (i % (SEQ // tm), 0))
        in_specs += [tab_spec] * 3
        args += list(rope_tabs)
    out = jax.ShapeDtypeStruct((rows, SEC), BF16)
    return pl.pallas_call(
        functools.partial(_inproj_ab_kernel, rope=rope),
        grid=(rows // tm,),
        in_specs=in_specs,
        out_specs=[sec_spec] * 5,
        out_shape=[out] * 5,
        compiler_params=_cparams("parallel"),
        name="inproj_ab_lat" if rope else "inproj_ab_ctx",
    )(*args)


def _attn_kernel(*refs, seg_lens, tq, lam_init):
    nseg = len(seg_lens)
    q_ref = refs[0]
    k_refs = refs[1:1 + nseg]
    v_refs = refs[1 + nseg:1 + 2 * nseg]
    lam_ref, g_ref, o_ref, vt_scr, s_even, s_odd, m_scr = refs[1 + 2 * nseg:]
    i = pl.program_id(2)
    chunks = []
    base = 0
    for seg, n_keys in enumerate(seg_lens):
        for c0 in range(0, n_keys, CK):
            chunks.append((seg, c0, base + c0, min(CK, n_keys - c0)))
        base += n_keys

    def fold(x, op):
        return op(x.reshape(x.shape[0] // 8, 8, x.shape[1]), axis=0)

    def step(s_write, s_read):
        q_t = q_ref[0].astype(F32).T
        sub = lax.broadcasted_iota(jnp.int32, (HEAD_W, tq), 0)
        zero = jnp.zeros_like(q_t)
        qq_t = jnp.concatenate([jnp.where(sub < QK_DIM, q_t, zero),
                                jnp.where(sub >= QK_DIM, q_t, zero)], axis=1).astype(BF16)
        if s_read is not None:
            m = m_scr[0:1, :]
            lsum = jnp.zeros((8, 2 * tq), F32)
            acc = jnp.zeros((HEAD_W, 2 * tq), F32)
        mx = None
        for seg, c0, r0, w in chunks:
            s = jnp.dot(k_refs[seg][0, c0:c0 + w, :], qq_t, preferred_element_type=F32)
            s_write[r0:r0 + w, :] = s
            t = fold(s, jnp.max)
            mx = t if mx is None else jnp.maximum(mx, t)
            if s_read is not None:
                p = jnp.exp(s_read[r0:r0 + w, :] - m)
                lsum = lsum + fold(p, jnp.sum)
                acc = acc + jnp.dot(vt_scr[:, r0:r0 + w], p.astype(BF16),
                                    preferred_element_type=F32)
        if s_read is not None:
            o_all = acc / jnp.sum(lsum, axis=0, keepdims=True)
            lam_t = lam_ref[...]
            e1 = jnp.exp(jnp.sum(lam_t[0:1, :] * lam_t[1:2, :], axis=-1, keepdims=True))
            e2 = jnp.exp(jnp.sum(lam_t[2:3, :] * lam_t[3:4, :], axis=-1, keepdims=True))
            lam = e1 - e2 + lam_init
            o_t = o_all[:, :tq] - lam * o_all[:, tq:]
            o_t = o_t * lax.rsqrt(jnp.mean(o_t * o_t, axis=0, keepdims=True) + EPS)
            o_t = o_t * g_ref[...] * (1.0 - lam_init)
            o_ref[0] = o_t.T.astype(BF16)
        m_scr[...] = jnp.broadcast_to(jnp.max(mx, axis=0, keepdims=True), m_scr.shape)

    @pl.when(i == 0)
    def _():
        for seg, c0, r0, w in chunks:
            vt_scr[:, r0:r0 + w] = v_refs[seg][0, c0:c0 + w, :].astype(F32).T.astype(BF16)
        step(s_even, None)

    @pl.when(jnp.logical_and(i > 0, i % 2 == 1))
    def _():
        step(s_odd, s_even)

    @pl.when(jnp.logical_and(i > 0, i % 2 == 0))
    def _():
        step(s_even, s_odd)


def _diff_attention(q, ks, vs, lam_tab, subln_g, lam_init):
    batch, nq, _ = q.shape
    tq = min(TQ, nq)
    nblk = nq // tq
    seg_lens = tuple(int(k.shape[1]) for k in ks)
    n_keys = sum(seg_lens)
    q_spec = pl.BlockSpec((1, tq, HEAD_W), lambda b, h, i: (b, jnp.minimum(i, nblk - 1), h))
    o_spec = pl.BlockSpec((1, tq, HEAD_W), lambda b, h, i: (b, jnp.maximum(i - 1, 0), h))
    kv_specs = [pl.BlockSpec((1, n, HEAD_W), lambda b, h, i: (b, 0, h)) for n in seg_lens]
    s_shape = pltpu.VMEM((n_keys, 2 * tq), F32)
    return pl.pallas_call(
        functools.partial(_attn_kernel, seg_lens=seg_lens, tq=tq, lam_init=lam_init),
        grid=(batch, HEADS, nblk + 1),
        in_specs=[q_spec] + kv_specs + kv_specs
                 + [pl.BlockSpec((N_COND, LANE), lambda b, h, i: (0, 0)),
                    pl.BlockSpec((HEAD_W, 1), lambda b, h, i: (0, 0))],
        out_specs=o_spec,
        out_shape=jax.ShapeDtypeStruct(q.shape, BF16),
        scratch_shapes=[pltpu.VMEM((HEAD_W, n_keys), BF16), s_shape, s_shape,
                        pltpu.VMEM((8, 2 * tq), F32)],
        compiler_params=_cparams("parallel", "parallel", "arbitrary"),
        name="diff_attn_%d" % nq,
    )(q, *ks, *vs, lam_tab, subln_g.reshape(HEAD_W, 1))


def _mix_epilogue(y, x_ref, gate_ref, pg_ref, o_ref):
    o_ref[...] = x_ref[...] + gate_ref[0] * (_rms(y) * pg_ref[...])


def _outproj_ab_kernel(a_ref, u_ref, vn_ref, wsp_ref, bsp_ref, wo_ref, x_ref, gate_ref, pg_ref,
                       o_ref, s_scr, *, tm):
    for c in range(tm // CHUNK):
        rs = slice(c * CHUNK, (c + 1) * CHUNK)
        for g in range(SEC // LANE):
            cs = slice(g * LANE, (g + 1) * LANE)
            sv = jnp.dot(wsp_ref[g], vn_ref[rs, cs], preferred_element_type=F32) + bsp_ref[g]
            s_scr[rs, cs] = (u_ref[rs, cs].astype(F32) * sv).astype(BF16)
    y = (jnp.dot(a_ref[...], wo_ref[0:SEC, :], preferred_element_type=F32)
         + jnp.dot(s_scr[...], wo_ref[SEC:2 * SEC, :], preferred_element_type=F32))
    _mix_epilogue(y, x_ref, gate_ref, pg_ref, o_ref)


def _outproj_ab(a2d, u2d, vn2d, w_sp, b_sp, w_out, x2d, mod3, layer, rowfn_maker, post_g):
    rows = x2d.shape[0]
    tm = TM
    rowfn = rowfn_maker(tm)
    sec_spec = pl.BlockSpec((tm, SEC), lambda i: (i, 0))
    row_spec = pl.BlockSpec((tm, D_MODEL), lambda i: (i, 0))
    return pl.pallas_call(
        functools.partial(_outproj_ab_kernel, tm=tm),
        grid=(rows // tm,),
        in_specs=[sec_spec, sec_spec, sec_spec, _resident(), _resident(), _resident(), row_spec,
                  _mod_spec(layer, 2, rowfn), pl.BlockSpec((1, D_MODEL), lambda i: (0, 0))],
        out_specs=row_spec,
        out_shape=jax.ShapeDtypeStruct((rows, D_MODEL), F32),
        scratch_shapes=[pltpu.VMEM((tm, SEC), BF16)],
        compiler_params=_cparams("parallel"),
        name="outproj_ab_%d" % rows,
    )(a2d, u2d, vn2d, w_sp, b_sp, w_out, x2d, mod3, post_g.reshape(1, D_MODEL))


def _outproj_cd_kernel(yc_ref, yd_ref, wo_ref, x_ref, gate_ref, pg_ref, o_ref):
    y = (jnp.dot(yc_ref[...], wo_ref[0:SEC, :], preferred_element_type=F32)
         + jnp.dot(yd_ref[...], wo_ref[SEC:2 * SEC, :], preferred_element_type=F32))
    _mix_epilogue(y, x_ref, gate_ref, pg_ref, o_ref)


def _outproj_cd(yc2d, yd2d, w_out, x2d, mod3, layer, rowfn_maker, post_g):
    rows = x2d.shape[0]
    tm = TM
    rowfn = rowfn_maker(tm)
    sec_spec = pl.BlockSpec((tm, SEC), lambda i: (i, 0))
    row_spec = pl.BlockSpec((tm, D_MODEL), lambda i: (i, 0))
    return pl.pallas_call(
        _outproj_cd_kernel,
        grid=(rows // tm,),
        in_specs=[sec_spec, sec_spec, _resident(), row_spec, _mod_spec(layer, 2, rowfn),
                  pl.BlockSpec((1, D_MODEL), lambda i: (0, 0))],
        out_specs=row_spec,
        out_shape=jax.ShapeDtypeStruct((rows, D_MODEL), F32),
        compiler_params=_cparams("parallel"),
        name="outproj_cd",
    )(yc2d, yd2d, w_out, x2d, mod3, post_g.reshape(1, D_MODEL))


def _ffn_kernel(x_ref, sh_ref, sc_ref, gate_ref, pg_ref, wg_ref, wu_ref, wd_ref, o_ref,
                hb_scr, acc_scr):
    j = pl.program_id(1)

    @pl.when(j == 0)
    def _():
        h = _rms(x_ref[...]) * (1.0 + sc_ref[0]) + sh_ref[0]
        hb_scr[...] = h.astype(BF16)
        acc_scr[...] = jnp.zeros_like(acc_scr)

    hb = hb_scr[...]
    g = jnp.dot(hb, wg_ref[0], preferred_element_type=F32)
    u = jnp.dot(hb, wu_ref[0], preferred_element_type=F32)
    a = (jax.nn.silu(g) * u).astype(BF16)
    acc_scr[...] += jnp.dot(a, wd_ref[0], preferred_element_type=F32)

    @pl.when(j == pl.num_programs(1) - 1)
    def _():
        _mix_epilogue(acc_scr[...], x_ref, gate_ref, pg_ref, o_ref)


def _ffn(x2d, mod3, layer, rowfn_maker, post_g, w_gate, w_up, w_down):
    rows = x2d.shape[0]
    tm = TM
    rowfn = rowfn_maker(tm)
    row_spec = pl.BlockSpec((tm, D_MODEL), lambda i, j: (i, 0))
    return pl.pallas_call(
        _ffn_kernel,
        grid=(rows // tm, D_FF // TF),
        in_specs=[row_spec, _mod_spec(layer, 3, rowfn), _mod_spec(layer, 4, rowfn),
                  _mod_spec(layer, 5, rowfn), pl.BlockSpec((1, D_MODEL), lambda i, j: (0, 0)),
                  pl.BlockSpec((1, D_MODEL, TF), lambda i, j: (layer, 0, j)),
                  pl.BlockSpec((1, D_MODEL, TF), lambda i, j: (layer, 0, j)),
                  pl.BlockSpec((1, TF, D_MODEL), lambda i, j: (layer, j, 0))],
        out_specs=row_spec,
        out_shape=jax.ShapeDtypeStruct((rows, D_MODEL), F32),
        scratch_shapes=[pltpu.VMEM((tm, D_MODEL), BF16), pltpu.VMEM((tm, D_MODEL), F32)],
        compiler_params=_cparams("parallel", "arbitrary"),
        name="ffn_%d" % rows,
    )(x2d, mod3, mod3, mod3, post_g.reshape(1, D_MODEL), w_gate, w_up, w_down)


def _inproj_cd_kernel(x_ref, sh_ref, sc_ref, w_ref, y_ref, f_ref):
    h = _rms(x_ref[...]) * (1.0 + sc_ref[0]) + sh_ref[0]
    hb = h.astype(BF16)

    def section(s):
        return jnp.dot(hb, w_ref[:, s * SEC:(s + 1) * SEC], preferred_element_type=F32)

    y_ref[...] = section(0) * jax.nn.sigmoid(section(1))
    f_ref[...] = section(2).astype(BF16)


def _inproj_cd(x2d, mod3, layer, rowfn_maker, w_in):
    rows = x2d.shape[0]
    tm = TM
    rowfn = rowfn_maker(tm)
    sec_spec = pl.BlockSpec((tm, SEC), lambda i: (i, 0))
    return pl.pallas_call(
        _inproj_cd_kernel,
        grid=(rows // tm,),
        in_specs=[pl.BlockSpec((tm, D_MODEL), lambda i: (i, 0)), _mod_spec(layer, 0, rowfn),
                  _mod_spec(layer, 1, rowfn), _resident()],
        out_specs=[sec_spec, sec_spec],
        out_shape=[jax.ShapeDtypeStruct((rows, SEC), F32), jax.ShapeDtypeStruct((rows, SEC), BF16)],
        compiler_params=_cparams("parallel"),
        name="inproj_cd",
    )(x2d, mod3, mod3, w_in)


def _conv_kernel(y_ref, prev_ref, next_ref, w_ref, b_ref, g_ref, be_ref, o_ref, ybuf, cbuf):
    i = pl.program_id(1)
    t = y_ref.shape[1]
    zeros = jnp.zeros((HALO, SEC), F32)
    ybuf[0:HALO, :] = jnp.where(i > 0, prev_ref[0], zeros)
    ybuf[HALO:HALO + t, :] = y_ref[0]
    ybuf[HALO + t:HALO + t + HALO, :] = jnp.where(i < pl.num_programs(1) - 1, next_ref[0], zeros)
    off = HALO - CONV_PAD

    def conv_block(rb, carry):
        r0 = pl.multiple_of(rb * CONV_RB, CONV_RB)
        for c in range(SEC // LANE):
            cs = slice(c * LANE, (c + 1) * LANE)
            win = ybuf[pl.ds(r0, CONV_RB + 2 * HALO), cs]
            acc = jnp.zeros((CONV_RB, LANE), F32) + b_ref[:, cs]
            n_win = CONV_RB + 2 * HALO
            for r in range(SUBLANE):
                shifted = win if r == 0 else pltpu.roll(win, n_win - r, 0)
                for a in range((2 * HALO) // SUBLANE):
                    w = a * SUBLANE + r - off
                    if 0 <= w < CONV_W:
                        acc = acc + (shifted[a * SUBLANE:a * SUBLANE + CONV_RB, :]
                                     * w_ref[w:w + 1, cs])
            cbuf[pl.ds(r0, CONV_RB), cs] = acc
        return carry

    lax.fori_loop(0, t // CONV_RB, conv_block, 0)

    def norm_block(rb, carry):
        r0 = pl.multiple_of(rb * CONV_RB, CONV_RB)
        v = cbuf[pl.ds(r0, CONV_RB), :]
        d = v - jnp.mean(v, axis=-1, keepdims=True)
        var = jnp.mean(d * d, axis=-1, keepdims=True)
        z = d * lax.rsqrt(var + EPS) * g_ref[...] + be_ref[...]
        o_ref[0, pl.ds(r0, CONV_RB), :] = jax.nn.silu(z).astype(BF16)
        return carry

    lax.fori_loop(0, t // CONV_RB, norm_block, 0)


def _conformer_conv(y3d, dw_w, dw_b, norm_g, norm_b):
    batch, n, _ = y3d.shape
    t = CONV_T
    hb = t // HALO
    nh = n // HALO
    vec = pl.BlockSpec((1, SEC), lambda b, i: (0, 0))
    return pl.pallas_call(
        _conv_kernel,
        grid=(batch, n // t),
        in_specs=[pl.BlockSpec((1, t, SEC), lambda b, i: (b, i, 0)),
                  pl.BlockSpec((1, HALO, SEC), lambda b, i: (b, jnp.maximum(i * hb - 1, 0), 0)),
                  pl.BlockSpec((1, HALO, SEC),
                               lambda b, i: (b, jnp.minimum((i + 1) * hb, nh - 1), 0)),
                  pl.BlockSpec((CONV_W, SEC), lambda b, i: (0, 0)), vec, vec, vec],
        out_specs=pl.BlockSpec((1, t, SEC), lambda b, i: (b, i, 0)),
        out_shape=jax.ShapeDtypeStruct((batch, n, SEC), BF16),
        scratch_shapes=[pltpu.VMEM((t + 2 * HALO, SEC), F32), pltpu.VMEM((t, SEC), F32)],
        compiler_params=_cparams("parallel", "arbitrary"),
        name="conformer_conv",
    )(y3d, y3d, y3d, dw_w, dw_b.reshape(1, SEC), norm_g.reshape(1, SEC), norm_b.reshape(1, SEC))


def _dft_constants():
    c = np.arange(LANE)
    ang_c = 2.0 * np.pi * np.outer(c, c) / LANE
    norm = 1.0 / math.sqrt(SEQ * LANE)
    chan = np.concatenate([np.cos(ang_c), -np.sin(ang_c)], axis=1) * norm
    r = np.arange(FFT_R)
    ang_r = 2.0 * np.pi * np.outer(r, r) / FFT_R
    cr, ci = np.cos(ang_r), -np.sin(ang_r)
    stage1 = np.block([[cr, -ci], [ci, cr]])
    stage2 = np.concatenate([cr, -ci], axis=1)
    ang_t = 2.0 * np.pi * np.outer(r, r) / SEQ
    return (jnp.asarray(chan, F32).astype(BF16), jnp.asarray(stage1, F32).astype(BF16),
            jnp.asarray(stage2, F32).astype(BF16),
            jnp.asarray(np.cos(ang_t), F32), jnp.asarray(-np.sin(ang_t), F32))


def _fft_a_kernel(x_ref, chan_ref, m1_ref, twc_ref, tws_ref, ar_ref, ai_ref):
    x = x_ref[0]
    nslab = x.shape[1] // LANE
    xs = jnp.concatenate([x[:, s * LANE:(s + 1) * LANE] for s in range(nslab)], axis=0)
    z = jnp.dot(xs, chan_ref[...], preferred_element_type=F32).astype(BF16)
    zr = jnp.concatenate([z[s * FFT_R:(s + 1) * FFT_R, 0:LANE] for s in range(nslab)], axis=1)
    zi = jnp.concatenate([z[s * FFT_R:(s + 1) * FFT_R, LANE:2 * LANE] for s in range(nslab)],
                         axis=1)
    a = jnp.dot(m1_ref[...], jnp.concatenate([zr, zi], axis=0), preferred_element_type=F32)
    a_r, a_i = a[0:FFT_R], a[FFT_R:2 * FFT_R]
    for j in range(x.shape[1] // SEC):
        cs = slice(j * SEC, (j + 1) * SEC)
        tc = jnp.concatenate([twc_ref[j]] * (SEC // LANE), axis=1)
        ts = jnp.concatenate([tws_ref[j]] * (SEC // LANE), axis=1)
        ar_ref[0, j] = (a_r[:, cs] * tc - a_i[:, cs] * ts).astype(BF16)
        ai_ref[0, j] = (a_r[:, cs] * ts + a_i[:, cs] * tc).astype(BF16)


def _fft_c_kernel(ar_ref, ai_ref, m2_ref, o_ref):
    a = jnp.concatenate([ar_ref[0], ai_ref[0]], axis=0)
    o_ref[0] = jnp.dot(m2_ref[...], a, preferred_element_type=F32).astype(BF16)


def _fourier_mix(f3d):
    batch, n, _ = f3d.shape
    chan, m1, m2, twc, tws = _dft_constants()
    twc = jnp.broadcast_to(twc[:, :, None], (FFT_R, FFT_R, LANE))
    tws = jnp.broadcast_to(tws[:, :, None], (FFT_R, FFT_R, LANE))
    nb = FFT_NB
    wide = nb * SEC
    xv = f3d.reshape(batch, FFT_R, FFT_R * SEC)
    blk = pl.BlockSpec((1, FFT_R, wide), lambda b, j: (b, 0, j))
    a_spec = pl.BlockSpec((1, nb, FFT_R, SEC), lambda b, j: (b, j, 0, 0))
    tw_spec = pl.BlockSpec((nb, FFT_R, LANE), lambda b, j: (j, 0, 0))
    a_shape = jax.ShapeDtypeStruct((batch, FFT_R, FFT_R, SEC), BF16)
    a_r, a_i = pl.pallas_call(
        _fft_a_kernel,
        grid=(batch, FFT_R // nb),
        in_specs=[blk, pl.BlockSpec((LANE, 2 * LANE), lambda b, j: (0, 0)),
                  pl.BlockSpec((2 * FFT_R, 2 * FFT_R), lambda b, j: (0, 0)), tw_spec, tw_spec],
        out_specs=[a_spec, a_spec],
        out_shape=[a_shape, a_shape],
        compiler_params=_cparams("parallel", "parallel"),
        name="fourier_stage_a",
    )(xv, chan, m1, twc, tws)
    out = pl.pallas_call(
        _fft_c_kernel,
        grid=(batch, FFT_R // nb),
        in_specs=[blk, blk, pl.BlockSpec((FFT_R, 2 * FFT_R), lambda b, j: (0, 0))],
        out_specs=blk,
        out_shape=jax.ShapeDtypeStruct((batch, FFT_R, FFT_R * SEC), BF16),
        compiler_params=_cparams("parallel", "parallel"),
        name="fourier_stage_c",
    )(a_r.reshape(batch, FFT_R, FFT_R * SEC), a_i.reshape(batch, FFT_R, FFT_R * SEC), m2)
    return out.reshape(batch, n, SEC)


def _rope_tables(n):
    rows = jnp.repeat(jnp.arange(n // GRID_W, dtype=F32), GRID_W)
    cols = jnp.tile(jnp.arange(GRID_W, dtype=F32), n // GRID_W)
    inv = ROPE_BASE ** (-jnp.arange(ROPE_FREQS, dtype=F32) / ROPE_FREQS)
    lane = np.arange(HEAD_W)
    freq = lane % ROPE_FREQS
    by_col = (lane % QK_DIM) // ROPE_HALF == 1
    second = (lane % ROPE_HALF) // ROPE_FREQS == 1
    ang = jnp.where(by_col[None, :], (cols[:, None] * inv)[:, freq], (rows[:, None] * inv)[:, freq])
    cos, sin = jnp.cos(ang), jnp.sin(ang)
    zero = jnp.zeros_like(sin)
    return cos, jnp.where(second[None, :], zero, -sin), jnp.where(second[None, :], sin, zero)


def kernel(x, c, ctx, c_ctx, mod_w, mod_b, post_mix_g, post_ffn_g, ffn_w_gate, ffn_w_up, ffn_w_down, ab_w_in, ab_w_out, ab_lam_q1, ab_lam_k1, ab_lam_q2, ab_lam_k2, ab_subln_g, ab_vnorm_g, ab_vnorm_b, ab_w_spatial, ab_b_spatial, cd_w_in, cd_w_out, cd_dw_w, cd_dw_b, cd_norm_g, cd_norm_b):
    batch, n, d = x.shape
    m = ctx.shape[1]
    depth = mod_w.shape[0]
    assert (n, d, m) == (SEQ, D_MODEL, CTX_LEN) and batch <= CTX_ROW

    cond = jnp.concatenate([c, c_ctx[None, :], jnp.zeros((N_COND - batch - 1, d), F32)], axis=0)
    mod3 = _modulation(cond, mod_w, mod_b).reshape(depth * N_COND * N_MOD, 1, d)
    rope_tabs = _rope_tables(n)
    wg, wu, wd = ffn_w_gate.astype(BF16), ffn_w_up.astype(BF16), ffn_w_down.astype(BF16)

    x_lat = x.reshape(batch * n, d)
    x_ctx = ctx.reshape(batch * m, d)
    for l in range(depth):
        last = l == depth - 1
        even = l % 2 == 0
        i = l // 2
        use_ctx = (not last) or even
        if even:
            lam_init = 0.8 - 0.6 * math.exp(-0.3 * l)
            w_in = ab_w_in[i].astype(BF16)
            w_out = ab_w_out[i].astype(BF16)
            w_sp = ab_w_spatial[i].astype(BF16)
            b_sp = ab_b_spatial[i].reshape(SEC // LANE, CHUNK, 1)
            lam_tab = jnp.zeros((N_COND, LANE), F32).at[0:4, 0:QK_DIM].set(
                jnp.stack([ab_lam_q1[i], ab_lam_k1[i], ab_lam_q2[i], ab_lam_k2[i]]))
            q, k, v, u, vn = _inproj_ab(x_lat, mod3, l, _lat_row, w_in, ab_vnorm_g[i],
                                        ab_vnorm_b[i], rope_tabs)
            qc, kc, vc, uc, vnc = _inproj_ab(x_ctx, mod3, l, _ctx_row, w_in, ab_vnorm_g[i],
                                             ab_vnorm_b[i], None)
            q3, k3, v3 = (t.reshape(batch, n, SEC) for t in (q, k, v))
            kc3, vc3 = kc.reshape(batch, m, SEC), vc.reshape(batch, m, SEC)
            a_lat = _diff_attention(q3, [k3, kc3], [v3, vc3], lam_tab, ab_subln_g[i], lam_init)
            x_lat_mix = _outproj_ab(a_lat.reshape(batch * n, SEC), u, vn, w_sp, b_sp, w_out, x_lat,
                                    mod3, l, _lat_row, post_mix_g[l])
            if not last:
                a_ctx = _diff_attention(qc.reshape(batch, m, SEC), [kc3], [vc3], lam_tab,
                                        ab_subln_g[i], lam_init)
                x_ctx_mix = _outproj_ab(a_ctx.reshape(batch * m, SEC), uc, vnc, w_sp, b_sp, w_out,
                                        x_ctx, mod3, l, _ctx_row, post_mix_g[l])
        else:
            w_in = cd_w_in[i].astype(BF16)
            w_out = cd_w_out[i].astype(BF16)

            def mix_cd(x2d, rows_per_seq, rowfn_maker):
                y, f = _inproj_cd(x2d, mod3, l, rowfn_maker, w_in)
                nb = x2d.shape[0] // rows_per_seq
                yc = _conformer_conv(y.reshape(nb, rows_per_seq, SEC), cd_dw_w[i], cd_dw_b[i],
                                     cd_norm_g[i], cd_norm_b[i])
                yd = _fourier_mix(f.reshape(nb, rows_per_seq, SEC))
                return _outproj_cd(yc.reshape(-1, SEC), yd.reshape(-1, SEC), w_out, x2d, mod3, l,
                                   rowfn_maker, post_mix_g[l])

            x_lat_mix = mix_cd(x_lat, n, _lat_row)
            if use_ctx:
                raise NotImplementedError("odd non-final layers are outside this problem's depth")
        x_lat = _ffn(x_lat_mix, mod3, l, _lat_row, post_ffn_g[l], wg, wu, wd)
        if not last:
            x_ctx = _ffn(x_ctx_mix, mod3, l, _ctx_row, post_ffn_g[l], wg, wu, wd)
    return x_lat.reshape(batch, n, d)
```

```python
import functools
import math

import numpy as np
import jax
import jax.numpy as jnp
from jax import lax
from jax.experimental import pallas as pl
from jax.experimental.pallas import tpu as pltpu

F32 = jnp.float32
BF16 = jnp.bfloat16

D_MODEL = 2048
SEQ = 4096
GRID_W = 64
CTX_LEN = 256
EPS = 1e-6
N_MOD = 6
D_FF = 5632
HEADS = 8
QK_DIM = 64
HEAD_W = 128
SEC = 1024
CHUNK = 128
CONV_W = 31
CONV_PAD = (CONV_W - 1) // 2
ROPE_HALF = 32
ROPE_FREQS = 16
ROPE_BASE = 10000.0
FFT_R = 64
N_COND = 8
CTX_ROW = 4

LANE = 128
SUBLANE = 8
VMEM_LIMIT = 56 * 1024 * 1024
TM = 512
TF = 512
TQ = 256
CK = 512
CONV_T = 512
CONV_RB = 64
HALO = 16
FFT_NB = 8


def _cparams(*sem):
    return pltpu.CompilerParams(dimension_semantics=sem, vmem_limit_bytes=VMEM_LIMIT)


def _resident():
    return pl.BlockSpec(memory_space=pltpu.VMEM)


def _mod_spec(layer, which, rowfn):
    return pl.BlockSpec((1, 1, D_MODEL),
                        lambda i, *_: ((layer * N_COND + rowfn(i)) * N_MOD + which, 0, 0))


def _lat_row(tm):
    return lambda i: i // (SEQ // tm)


def _ctx_row(_tm):
    return lambda i: CTX_ROW


def _rms(x):
    return x * lax.rsqrt(jnp.mean(x * x, axis=-1, keepdims=True) + EPS)


def _mod_kernel(c_ref, w_ref, b_ref, o_ref):
    s = jax.nn.silu(c_ref[...]).astype(BF16)
    o_ref[0] = jnp.dot(s, w_ref[0].astype(BF16), preferred_element_type=F32) + b_ref[0]


def _modulation(cond, mod_w, mod_b):
    depth, _, n = mod_w.shape
    tn = 1024
    return pl.pallas_call(
        _mod_kernel,
        grid=(depth, n // tn),
        in_specs=[pl.BlockSpec((N_COND, D_MODEL), lambda l, j: (0, 0)),
                  pl.BlockSpec((1, D_MODEL, tn), lambda l, j: (l, 0, j)),
                  pl.BlockSpec((1, 1, tn), lambda l, j: (l, 0, j))],
        out_specs=pl.BlockSpec((1, N_COND, tn), lambda l, j: (l, 0, j)),
        out_shape=jax.ShapeDtypeStruct((depth, N_COND, n), F32),
        compiler_params=_cparams("parallel", "parallel"),
        name="adaln_mod",
    )(cond, mod_w, mod_b.reshape(depth, 1, n))


def _inproj_ab_kernel(*refs, rope):
    if rope:
        (x_ref, sh_ref, sc_ref, w_ref, vg_ref, vb_ref, cos_ref, s1_ref, s2_ref,
         q_ref, k_ref, v_ref, u_ref, vn_ref) = refs
    else:
        x_ref, sh_ref, sc_ref, w_ref, vg_ref, vb_ref, q_ref, k_ref, v_ref, u_ref, vn_ref = refs
    h = _rms(x_ref[...]) * (1.0 + sc_ref[0]) + sh_ref[0]
    hb = h.astype(BF16)

    def section(s):
        return jnp.dot(hb, w_ref[:, s * SEC:(s + 1) * SEC], preferred_element_type=F32)

    def rotate(z):
        if not rope:
            return z
        cos, s1, s2 = cos_ref[...], s1_ref[...], s2_ref[...]
        outs = []
        for hd in range(HEADS):
            zs = z[:, hd * HEAD_W:(hd + 1) * HEAD_W]
            outs.append(zs * cos + pltpu.roll(zs, HEAD_W - ROPE_FREQS, 1) * s1
                        + pltpu.roll(zs, ROPE_FREQS, 1) * s2)
        return jnp.concatenate(outs, axis=1)

    g = jax.nn.gelu(section(4))
    outs = []
    for gi in range(SEC // LANE):
        sl = slice(gi * LANE, (gi + 1) * LANE)
        gs = g[:, sl]
        d = gs - jnp.mean(gs, axis=-1, keepdims=True)
        var = jnp.mean(d * d, axis=-1, keepdims=True)
        outs.append(d * lax.rsqrt(var + EPS) * vg_ref[:, sl] + vb_ref[:, sl])
    vn_ref[...] = jnp.concatenate(outs, axis=1).astype(BF16)
    u_ref[...] = jax.nn.gelu(section(3)).astype(BF16)
    q_ref[...] = (rotate(section(0)) * (QK_DIM ** -0.5)).astype(BF16)
    k_ref[...] = rotate(section(1)).astype(BF16)
    v_ref[...] = section(2).astype(BF16)


def _inproj_ab(x2d, mod3, layer, rowfn_maker, w_in, vnorm_g, vnorm_b, rope_tabs):
    rows = x2d.shape[0]
    tm = TM
    rowfn = rowfn_maker(tm)
    rope = rope_tabs is not None
    row_spec = pl.BlockSpec((tm, D_MODEL), lambda i: (i, 0))
    sec_spec = pl.BlockSpec((tm, SEC), lambda i: (i, 0))
    vec_spec = pl.BlockSpec((1, SEC), lambda i: (0, 0))
    in_specs = [row_spec, _mod_spec(layer, 0, rowfn), _mod_spec(layer, 1, rowfn), _resident(),
                vec_spec, vec_spec]
    args = [x2d, mod3, mod3, w_in, vnorm_g.reshape(1, SEC), vnorm_b.reshape(1, SEC)]
    if rope:
        tab_spec = pl.BlockSpec((tm, HEAD_W), lambda i: (i % (SEQ // tm), 0))
        in_specs += [tab_spec] * 3
        args += list(rope_tabs)
    out = jax.ShapeDtypeStruct((rows, SEC), BF16)
    return pl.pallas_call(
        functools.partial(_inproj_ab_kernel, rope=rope),
        grid=(rows // tm,),
        in_specs=in_specs,
        out_specs=[sec_spec] * 5,
        out_shape=[out] * 5,
        compiler_params=_cparams("parallel"),
        name="inproj_ab_lat" if rope else "inproj_ab_ctx",
    )(*args)


def _attn_kernel(*refs, seg_lens, tq, lam_init):
    nseg = len(seg_lens)
    q_ref = refs[0]
    k_refs = refs[1:1 + nseg]
    v_refs = refs[1 + nseg:1 + 2 * nseg]
    lam_ref, g_ref, o_ref, vt_scr, s_even, s_odd, m_scr = refs[1 + 2 * nseg:]
    i = pl.program_id(2)
    chunks = []
    base = 0
    for seg, n_keys in enumerate(seg_lens):
        for c0 in range(0, n_keys, CK):
            chunks.append((seg, c0, base + c0, min(CK, n_keys - c0)))
        base += n_keys

    def fold(x, op):
        return op(x.reshape(x.shape[0] // 8, 8, x.shape[1]), axis=0)

    def step(s_write, s_read):
        q_t = q_ref[0].astype(F32).T
        sub = lax.broadcasted_iota(jnp.int32, (HEAD_W, tq), 0)
        zero = jnp.zeros_like(q_t)
        qq_t = jnp.concatenate([jnp.where(sub < QK_DIM, q_t, zero),
                                jnp.where(sub >= QK_DIM, q_t, zero)], axis=1).astype(BF16)
        if s_read is not None:
            m = m_scr[0:1, :]
            lsum = jnp.zeros((8, 2 * tq), F32)
            acc = jnp.zeros((HEAD_W, 2 * tq), F32)
        mx = None
        for seg, c0, r0, w in chunks:
            s = jnp.dot(k_refs[seg][0, c0:c0 + w, :], qq_t, preferred_element_type=F32)
            s_write[r0:r0 + w, :] = s
            t = fold(s, jnp.max)
            mx = t if mx is None else jnp.maximum(mx, t)
            if s_read is not None:
                p = jnp.exp(s_read[r0:r0 + w, :] - m)
                lsum = lsum + fold(p, jnp.sum)
                acc = acc + jnp.dot(vt_scr[:, r0:r0 + w], p.astype(BF16),
                                    preferred_element_type=F32)
        if s_read is not None:
            o_all = acc / jnp.sum(lsum, axis=0, keepdims=True)
            lam_t = lam_ref[...]
            e1 = jnp.exp(jnp.sum(lam_t[0:1, :] * lam_t[1:2, :], axis=-1, keepdims=True))
            e2 = jnp.exp(jnp.sum(lam_t[2:3, :] * lam_t[3:4, :], axis=-1, keepdims=True))
            lam = e1 - e2 + lam_init
            o_t = o_all[:, :tq] - lam * o_all[:, tq:]
            o_t = o_t * lax.rsqrt(jnp.mean(o_t * o_t, axis=0, keepdims=True) + EPS)
            o_t = o_t * g_ref[...] * (1.0 - lam_init)
            o_ref[0] = o_t.T.astype(BF16)
        m_scr[...] = jnp.broadcast_to(jnp.max(mx, axis=0, keepdims=True), m_scr.shape)

    @pl.when(i == 0)
    def _():
        for seg, c0, r0, w in chunks:
            vt_scr[:, r0:r0 + w] = v_refs[seg][0, c0:c0 + w, :].astype(F32).T.astype(BF16)
        step(s_even, None)

    @pl.when(jnp.logical_and(i > 0, i % 2 == 1))
    def _():
        step(s_odd, s_even)

    @pl.when(jnp.logical_and(i > 0, i % 2 == 0))
    def _():
        step(s_even, s_odd)


def _diff_attention(q, ks, vs, lam_tab, subln_g, lam_init):
    batch, nq, _ = q.shape
    tq = min(TQ, nq)
    nblk = nq // tq
    seg_lens = tuple(int(k.shape[1]) for k in ks)
    n_keys = sum(seg_lens)
    q_spec = pl.BlockSpec((1, tq, HEAD_W), lambda b, h, i: (b, jnp.minimum(i, nblk - 1), h))
    o_spec = pl.BlockSpec((1, tq, HEAD_W), lambda b, h, i: (b, jnp.maximum(i - 1, 0), h))
    kv_specs = [pl.BlockSpec((1, n, HEAD_W), lambda b, h, i: (b, 0, h)) for n in seg_lens]
    s_shape = pltpu.VMEM((n_keys, 2 * tq), F32)
    return pl.pallas_call(
        functools.partial(_attn_kernel, seg_lens=seg_lens, tq=tq, lam_init=lam_init),
        grid=(batch, HEADS, nblk + 1),
        in_specs=[q_spec] + kv_specs + kv_specs
                 + [pl.BlockSpec((N_COND, LANE), lambda b, h, i: (0, 0)),
                    pl.BlockSpec((HEAD_W, 1), lambda b, h, i: (0, 0))],
        out_specs=o_spec,
        out_shape=jax.ShapeDtypeStruct(q.shape, BF16),
        scratch_shapes=[pltpu.VMEM((HEAD_W, n_keys), BF16), s_shape, s_shape,
                        pltpu.VMEM((8, 2 * tq), F32)],
        compiler_params=_cparams("parallel", "parallel", "arbitrary"),
        name="diff_attn_%d" % nq,
    )(q, *ks, *vs, lam_tab, subln_g.reshape(HEAD_W, 1))


def _mix_epilogue(y, x_ref, gate_ref, pg_ref, o_ref):
    o_ref[...] = x_ref[...] + gate_ref[0] * (_rms(y) * pg_ref[...])


def _outproj_ab_kernel(a_ref, u_ref, vn_ref, wsp_ref, bsp_ref, wo_ref, x_ref, gate_ref, pg_ref,
                       o_ref, s_scr, *, tm):
    y = jnp.dot(a_ref[...], wo_ref[0:SEC, :], preferred_element_type=F32)
    for c in range(tm // CHUNK):
        rs = slice(c * CHUNK, (c + 1) * CHUNK)
        for g in range(SEC // LANE):
            cs = slice(g * LANE, (g + 1) * LANE)
            sv = jnp.dot(wsp_ref[g], vn_ref[rs, cs], preferred_element_type=F32) + bsp_ref[g]
            s_scr[rs, cs] = (u_ref[rs, cs].astype(F32) * sv).astype(BF16)
    y = y + jnp.dot(s_scr[...], wo_ref[SEC:2 * SEC, :], preferred_element_type=F32)
    _mix_epilogue(y, x_ref, gate_ref, pg_ref, o_ref)


def _outproj_ab(a2d, u2d, vn2d, w_sp, b_sp, w_out, x2d, mod3, layer, rowfn_maker, post_g):
    rows = x2d.shape[0]
    tm = TM
    rowfn = rowfn_maker(tm)
    sec_spec = pl.BlockSpec((tm, SEC), lambda i: (i, 0))
    row_spec = pl.BlockSpec((tm, D_MODEL), lambda i: (i, 0))
    return pl.pallas_call(
        functools.partial(_outproj_ab_kernel, tm=tm),
        grid=(rows // tm,),
        in_specs=[sec_spec, sec_spec, sec_spec, _resident(), _resident(), _resident(), row_spec,
                  _mod_spec(layer, 2, rowfn), pl.BlockSpec((1, D_MODEL), lambda i: (0, 0))],
        out_specs=row_spec,
        out_shape=jax.ShapeDtypeStruct((rows, D_MODEL), F32),
        scratch_shapes=[pltpu.VMEM((tm, SEC), BF16)],
        compiler_params=_cparams("parallel"),
        name="outproj_ab_%d" % rows,
    )(a2d, u2d, vn2d, w_sp, b_sp, w_out, x2d, mod3, post_g.reshape(1, D_MODEL))


def _outproj_cd_kernel(yc_ref, yd_ref, wo_ref, x_ref, gate_ref, pg_ref, o_ref):
    y = (jnp.dot(yc_ref[...], wo_ref[0:SEC, :], preferred_element_type=F32)
         + jnp.dot(yd_ref[...], wo_ref[SEC:2 * SEC, :], preferred_element_type=F32))
    _mix_epilogue(y, x_ref, gate_ref, pg_ref, o_ref)


def _outproj_cd(yc2d, yd2d, w_out, x2d, mod3, layer, rowfn_maker, post_g):
    rows = x2d.shape[0]
    tm = TM
    rowfn = rowfn_maker(tm)
    sec_spec = pl.BlockSpec((tm, SEC), lambda i: (i, 0))
    row_spec = pl.BlockSpec((tm, D_MODEL), lambda i: (i, 0))
    return pl.pallas_call(
        _outproj_cd_kernel,
        grid=(rows // tm,),
        in_specs=[sec_spec, sec_spec, _resident(), row_spec, _mod_spec(layer, 2, rowfn),
                  pl.BlockSpec((1, D_MODEL), lambda i: (0, 0))],
        out_specs=row_spec,
        out_shape=jax.ShapeDtypeStruct((rows, D_MODEL), F32),
        compiler_params=_cparams("parallel"),
        name="outproj_cd",
    )(yc2d, yd2d, w_out, x2d, mod3, post_g.reshape(1, D_MODEL))


def _ffn_kernel(xp_ref, xn_ref, sh0_ref, sc0_ref, shn_ref, scn_ref, gate_ref, pg_ref,
                wg_ref, wu_ref, wd_ref, o_ref, hb_scr, acc_scr, *, nrow):
    i, j = pl.program_id(0), pl.program_id(1)
    nj = pl.num_programs(1)
    cur, prev = i % 2, (i + 1) % 2

    def prologue(x_ref, sh_ref, sc_ref, slot):
        h = _rms(x_ref[...]) * (1.0 + sc_ref[0]) + sh_ref[0]
        hb_scr[slot] = h.astype(BF16)

    def chunk(first):
        hb = hb_scr[cur]
        g = jnp.dot(hb, wg_ref[0], preferred_element_type=F32)
        u = jnp.dot(hb, wu_ref[0], preferred_element_type=F32)
        a = (jax.nn.silu(g) * u).astype(BF16)
        d = jnp.dot(a, wd_ref[0], preferred_element_type=F32)
        acc_scr[cur] = d if first else acc_scr[cur] + d

    def epilogue():
        _mix_epilogue(acc_scr[prev], xp_ref, gate_ref, pg_ref, o_ref)

    @pl.when(jnp.logical_and(i == 0, j == 0))
    def _():
        prologue(xp_ref, sh0_ref, sc0_ref, 0)
        chunk(True)

    @pl.when(jnp.logical_and(jnp.logical_and(i > 0, i < nrow), j == 0))
    def _():
        epilogue()
        chunk(True)

    @pl.when(jnp.logical_and(i < nrow - 1, j == nj - 1))
    def _():
        chunk(False)
        prologue(xn_ref, shn_ref, scn_ref, prev)

    @pl.when(jnp.logical_and(jnp.logical_and(i < nrow, j > 0),
                             jnp.logical_or(j < nj - 1, i == nrow - 1)))
    def _():
        chunk(False)

    @pl.when(jnp.logical_and(i == nrow, j == 0))
    def _():
        epilogue()


def _ffn(x2d, mod3, layer, rowfn_maker, post_g, w_gate, w_up, w_down):
    rows = x2d.shape[0]
    tm = TM
    nrow = rows // tm
    nj = D_FF // TF
    rowfn = rowfn_maker(tm)

    def prev_tile(i):
        return jnp.maximum(i - 1, 0)

    def next_tile(i):
        return jnp.minimum(i + 1, nrow - 1)

    def wcol(i, j):
        return jnp.where(i < nrow, j, nj - 1)

    return pl.pallas_call(
        functools.partial(_ffn_kernel, nrow=nrow),
        grid=(nrow + 1, nj),
        in_specs=[pl.BlockSpec((tm, D_MODEL), lambda i, j: (prev_tile(i), 0)),
                  pl.BlockSpec((tm, D_MODEL), lambda i, j: (next_tile(i), 0)),
                  _mod_spec(layer, 3, lambda i: rowfn(0)), _mod_spec(layer, 4, lambda i: rowfn(0)),
                  _mod_spec(layer, 3, lambda i: rowfn(next_tile(i))),
                  _mod_spec(layer, 4, lambda i: rowfn(next_tile(i))),
                  _mod_spec(layer, 5, lambda i: rowfn(prev_tile(i))),
                  pl.BlockSpec((1, D_MODEL), lambda i, j: (0, 0)),
                  pl.BlockSpec((1, D_MODEL, TF), lambda i, j: (layer, 0, wcol(i, j))),
                  pl.BlockSpec((1, D_MODEL, TF), lambda i, j: (layer, 0, wcol(i, j))),
                  pl.BlockSpec((1, TF, D_MODEL), lambda i, j: (layer, wcol(i, j), 0))],
        out_specs=pl.BlockSpec((tm, D_MODEL), lambda i, j: (prev_tile(i), 0)),
        out_shape=jax.ShapeDtypeStruct((rows, D_MODEL), F32),
        scratch_shapes=[pltpu.VMEM((2, tm, D_MODEL), BF16), pltpu.VMEM((2, tm, D_MODEL), F32)],
        compiler_params=_cparams("arbitrary", "arbitrary"),
        name="ffn_%d" % rows,
    )(x2d, x2d, mod3, mod3, mod3, mod3, mod3, post_g.reshape(1, D_MODEL), w_gate, w_up, w_down)


def _inproj_cd_kernel(x_ref, sh_ref, sc_ref, w_ref, y_ref, f_ref):
    h = _rms(x_ref[...]) * (1.0 + sc_ref[0]) + sh_ref[0]
    hb = h.astype(BF16)

    def section(s):
        return jnp.dot(hb, w_ref[:, s * SEC:(s + 1) * SEC], preferred_element_type=F32)

    y_ref[...] = section(0) * jax.nn.sigmoid(section(1))
    f_ref[...] = section(2).astype(BF16)


def _inproj_cd(x2d, mod3, layer, rowfn_maker, w_in):
    rows = x2d.shape[0]
    tm = TM
    rowfn = rowfn_maker(tm)
    sec_spec = pl.BlockSpec((tm, SEC), lambda i: (i, 0))
    return pl.pallas_call(
        _inproj_cd_kernel,
        grid=(rows // tm,),
        in_specs=[pl.BlockSpec((tm, D_MODEL), lambda i: (i, 0)), _mod_spec(layer, 0, rowfn),
                  _mod_spec(layer, 1, rowfn), _resident()],
        out_specs=[sec_spec, sec_spec],
        out_shape=[jax.ShapeDtypeStruct((rows, SEC), F32), jax.ShapeDtypeStruct((rows, SEC), BF16)],
        compiler_params=_cparams("parallel"),
        name="inproj_cd",
    )(x2d, mod3, mod3, w_in)


def _conv_kernel(y_ref, prev_ref, next_ref, w_ref, b_ref, g_ref, be_ref, o_ref, ybuf, cbuf):
    i = pl.program_id(1)
    t = y_ref.shape[1]
    zeros = jnp.zeros((HALO, SEC), F32)
    ybuf[0:HALO, :] = jnp.where(i > 0, prev_ref[0], zeros)
    ybuf[HALO:HALO + t, :] = y_ref[0]
    ybuf[HALO + t:HALO + t + HALO, :] = jnp.where(i < pl.num_programs(1) - 1, next_ref[0], zeros)
    off = HALO - CONV_PAD

    def conv_block(rb, carry):
        r0 = pl.multiple_of(rb * CONV_RB, CONV_RB)
        for c in range(SEC // LANE):
            cs = slice(c * LANE, (c + 1) * LANE)
            win = ybuf[pl.ds(r0, CONV_RB + 2 * HALO), cs]
            acc = jnp.zeros((CONV_RB, LANE), F32) + b_ref[:, cs]
            n_win = CONV_RB + 2 * HALO
            for r in range(SUBLANE):
                shifted = win if r == 0 else pltpu.roll(win, n_win - r, 0)
                for a in range((2 * HALO) // SUBLANE):
                    w = a * SUBLANE + r - off
                    if 0 <= w < CONV_W:
                        acc = acc + (shifted[a * SUBLANE:a * SUBLANE + CONV_RB, :]
                                     * w_ref[w:w + 1, cs])
            cbuf[pl.ds(r0, CONV_RB), cs] = acc
        return carry

    lax.fori_loop(0, t // CONV_RB, conv_block, 0)

    def norm_block(rb, carry):
        r0 = pl.multiple_of(rb * CONV_RB, CONV_RB)
        v = cbuf[pl.ds(r0, CONV_RB), :]
        d = v - jnp.mean(v, axis=-1, keepdims=True)
        var = jnp.mean(d * d, axis=-1, keepdims=True)
        z = d * lax.rsqrt(var + EPS) * g_ref[...] + be_ref[...]
        o_ref[0, pl.ds(r0, CONV_RB), :] = jax.nn.silu(z).astype(BF16)
        return carry

    lax.fori_loop(0, t // CONV_RB, norm_block, 0)


def _conformer_conv(y3d, dw_w, dw_b, norm_g, norm_b):
    batch, n, _ = y3d.shape
    t = CONV_T
    hb = t // HALO
    nh = n // HALO
    vec = pl.BlockSpec((1, SEC), lambda b, i: (0, 0))
    return pl.pallas_call(
        _conv_kernel,
        grid=(batch, n // t),
        in_specs=[pl.BlockSpec((1, t, SEC), lambda b, i: (b, i, 0)),
                  pl.BlockSpec((1, HALO, SEC), lambda b, i: (b, jnp.maximum(i * hb - 1, 0), 0)),
                  pl.BlockSpec((1, HALO, SEC),
                               lambda b, i: (b, jnp.minimum((i + 1) * hb, nh - 1), 0)),
                  pl.BlockSpec((CONV_W, SEC), lambda b, i: (0, 0)), vec, vec, vec],
        out_specs=pl.BlockSpec((1, t, SEC), lambda b, i: (b, i, 0)),
        out_shape=jax.ShapeDtypeStruct((batch, n, SEC), BF16),
        scratch_shapes=[pltpu.VMEM((t + 2 * HALO, SEC), F32), pltpu.VMEM((t, SEC), F32)],
        compiler_params=_cparams("parallel", "arbitrary"),
        name="conformer_conv",
    )(y3d, y3d, y3d, dw_w, dw_b.reshape(1, SEC), norm_g.reshape(1, SEC), norm_b.reshape(1, SEC))


def _dft_constants():
    c = np.arange(LANE)
    ang_c = 2.0 * np.pi * np.outer(c, c) / LANE
    norm = 1.0 / math.sqrt(SEQ * LANE)
    chan = np.concatenate([np.cos(ang_c), -np.sin(ang_c)], axis=1) * norm
    r = np.arange(FFT_R)
    ang_r = 2.0 * np.pi * np.outer(r, r) / FFT_R
    cr, ci = np.cos(ang_r), -np.sin(ang_r)
    stage1 = np.block([[cr, -ci], [ci, cr]])
    stage2 = np.concatenate([cr, -ci], axis=1)
    ang_t = 2.0 * np.pi * np.outer(r, r) / SEQ
    return (jnp.asarray(chan, F32).astype(BF16), jnp.asarray(stage1, F32).astype(BF16),
            jnp.asarray(stage2, F32).astype(BF16),
            jnp.asarray(np.cos(ang_t), F32), jnp.asarray(-np.sin(ang_t), F32))


def _fft_a_kernel(x_ref, chan_ref, m1_ref, twc_ref, tws_ref, ar_ref, ai_ref):
    x = x_ref[0]
    nslab = x.shape[1] // LANE
    xs = jnp.concatenate([x[:, s * LANE:(s + 1) * LANE] for s in range(nslab)], axis=0)
    z = jnp.dot(xs, chan_ref[...], preferred_element_type=F32).astype(BF16)
    zr = jnp.concatenate([z[s * FFT_R:(s + 1) * FFT_R, 0:LANE] for s in range(nslab)], axis=1)
    zi = jnp.concatenate([z[s * FFT_R:(s + 1) * FFT_R, LANE:2 * LANE] for s in range(nslab)],
                         axis=1)
    a = jnp.dot(m1_ref[...], jnp.concatenate([zr, zi], axis=0), preferred_element_type=F32)
    a_r, a_i = a[0:FFT_R], a[FFT_R:2 * FFT_R]
    for j in range(x.shape[1] // SEC):
        cs = slice(j * SEC, (j + 1) * SEC)
        tc = jnp.concatenate([twc_ref[j]] * (SEC // LANE), axis=1)
        ts = jnp.concatenate([tws_ref[j]] * (SEC // LANE), axis=1)
        ar_ref[0, j] = (a_r[:, cs] * tc - a_i[:, cs] * ts).astype(BF16)
        ai_ref[0, j] = (a_r[:, cs] * ts + a_i[:, cs] * tc).astype(BF16)


def _fft_c_kernel(ar_ref, ai_ref, m2_ref, o_ref):
    a = jnp.concatenate([ar_ref[0], ai_ref[0]], axis=0)
    o_ref[0] = jnp.dot(m2_ref[...], a, preferred_element_type=F32).astype(BF16)


def _fourier_mix(f3d):
    batch, n, _ = f3d.shape
    chan, m1, m2, twc, tws = _dft_constants()
    twc = jnp.broadcast_to(twc[:, :, None], (FFT_R, FFT_R, LANE))
    tws = jnp.broadcast_to(tws[:, :, None], (FFT_R, FFT_R, LANE))
    nb = FFT_NB
    wide = nb * SEC
    xv = f3d.reshape(batch, FFT_R, FFT_R * SEC)
    blk = pl.BlockSpec((1, FFT_R, wide), lambda b, j: (b, 0, j))
    a_spec = pl.BlockSpec((1, nb, FFT_R, SEC), lambda b, j: (b, j, 0, 0))
    tw_spec = pl.BlockSpec((nb, FFT_R, LANE), lambda b, j: (j, 0, 0))
    a_shape = jax.ShapeDtypeStruct((batch, FFT_R, FFT_R, SEC), BF16)
    a_r, a_i = pl.pallas_call(
        _fft_a_kernel,
        grid=(batch, FFT_R // nb),
        in_specs=[blk, pl.BlockSpec((LANE, 2 * LANE), lambda b, j: (0, 0)),
                  pl.BlockSpec((2 * FFT_R, 2 * FFT_R), lambda b, j: (0, 0)), tw_spec, tw_spec],
        out_specs=[a_spec, a_spec],
        out_shape=[a_shape, a_shape],
        compiler_params=_cparams("parallel", "parallel"),
        name="fourier_stage_a",
    )(xv, chan, m1, twc, tws)
    out = pl.pallas_call(
        _fft_c_kernel,
        grid=(batch, FFT_R // nb),
        in_specs=[blk, blk, pl.BlockSpec((FFT_R, 2 * FFT_R), lambda b, j: (0, 0))],
        out_specs=blk,
        out_shape=jax.ShapeDtypeStruct((batch, FFT_R, FFT_R * SEC), BF16),
        compiler_params=_cparams("parallel", "parallel"),
        name="fourier_stage_c",
    )(a_r.reshape(batch, FFT_R, FFT_R * SEC), a_i.reshape(batch, FFT_R, FFT_R * SEC), m2)
    return out.reshape(batch, n, SEC)


def _rope_tables(n):
    rows = jnp.repeat(jnp.arange(n // GRID_W, dtype=F32), GRID_W)
    cols = jnp.tile(jnp.arange(GRID_W, dtype=F32), n // GRID_W)
    inv = ROPE_BASE ** (-jnp.arange(ROPE_FREQS, dtype=F32) / ROPE_FREQS)
    lane = np.arange(HEAD_W)
    freq = lane % ROPE_FREQS
    by_col = (lane % QK_DIM) // ROPE_HALF == 1
    second = (lane % ROPE_HALF) // ROPE_FREQS == 1
    ang = jnp.where(by_col[None, :], (cols[:, None] * inv)[:, freq], (rows[:, None] * inv)[:, freq])
    cos, sin = jnp.cos(ang), jnp.sin(ang)
    zero = jnp.zeros_like(sin)
    return cos, jnp.where(second[None, :], zero, -sin), jnp.where(second[None, :], sin, zero)


def kernel(x, c, ctx, c_ctx, mod_w, mod_b, post_mix_g, post_ffn_g, ffn_w_gate, ffn_w_up, ffn_w_down, ab_w_in, ab_w_out, ab_lam_q1, ab_lam_k1, ab_lam_q2, ab_lam_k2, ab_subln_g, ab_vnorm_g, ab_vnorm_b, ab_w_spatial, ab_b_spatial, cd_w_in, cd_w_out, cd_dw_w, cd_dw_b, cd_norm_g, cd_norm_b):
    batch, n, d = x.shape
    m = ctx.shape[1]
    depth = mod_w.shape[0]
    assert (n, d, m) == (SEQ, D_MODEL, CTX_LEN) and batch <= CTX_ROW

    cond = jnp.concatenate([c, c_ctx[None, :], jnp.zeros((N_COND - batch - 1, d), F32)], axis=0)
    mod3 = _modulation(cond, mod_w, mod_b).reshape(depth * N_COND * N_MOD, 1, d)
    rope_tabs = _rope_tables(n)
    wg, wu, wd = ffn_w_gate.astype(BF16), ffn_w_up.astype(BF16), ffn_w_down.astype(BF16)

    x_lat = x.reshape(batch * n, d)
    x_ctx = ctx.reshape(batch * m, d)
    for l in range(depth):
        last = l == depth - 1
        even = l % 2 == 0
        i = l // 2
        use_ctx = (not last) or even
        if even:
            lam_init = 0.8 - 0.6 * math.exp(-0.3 * l)
            w_in = ab_w_in[i].astype(BF16)
            w_out = ab_w_out[i].astype(BF16)
            w_sp = ab_w_spatial[i].astype(BF16)
            b_sp = ab_b_spatial[i].reshape(SEC // LANE, CHUNK, 1)
            lam_tab = jnp.zeros((N_COND, LANE), F32).at[0:4, 0:QK_DIM].set(
                jnp.stack([ab_lam_q1[i], ab_lam_k1[i], ab_lam_q2[i], ab_lam_k2[i]]))
            q, k, v, u, vn = _inproj_ab(x_lat, mod3, l, _lat_row, w_in, ab_vnorm_g[i],
                                        ab_vnorm_b[i], rope_tabs)
            qc, kc, vc, uc, vnc = _inproj_ab(x_ctx, mod3, l, _ctx_row, w_in, ab_vnorm_g[i],
                                             ab_vnorm_b[i], None)
            q3, k3, v3 = (t.reshape(batch, n, SEC) for t in (q, k, v))
            kc3, vc3 = kc.reshape(batch, m, SEC), vc.reshape(batch, m, SEC)
            a_lat = _diff_attention(q3, [k3, kc3], [v3, vc3], lam_tab, ab_subln_g[i], lam_init)
            x_lat_mix = _outproj_ab(a_lat.reshape(batch * n, SEC), u, vn, w_sp, b_sp, w_out, x_lat,
                                    mod3, l, _lat_row, post_mix_g[l])
            if not last:
                a_ctx = _diff_attention(qc.reshape(batch, m, SEC), [kc3], [vc3], lam_tab,
                                        ab_subln_g[i], lam_init)
                x_ctx_mix = _outproj_ab(a_ctx.reshape(batch * m, SEC), uc, vnc, w_sp, b_sp, w_out,
                                        x_ctx, mod3, l, _ctx_row, post_mix_g[l])
        else:
            w_in = cd_w_in[i].astype(BF16)
            w_out = cd_w_out[i].astype(BF16)

            def mix_cd(x2d, rows_per_seq, rowfn_maker):
                y, f = _inproj_cd(x2d, mod3, l, rowfn_maker, w_in)
                nb = x2d.shape[0] // rows_per_seq
                yc = _conformer_conv(y.reshape(nb, rows_per_seq, SEC), cd_dw_w[i], cd_dw_b[i],
                                     cd_norm_g[i], cd_norm_b[i])
                yd = _fourier_mix(f.reshape(nb, rows_per_seq, SEC))
                return _outproj_cd(yc.reshape(-1, SEC), yd.reshape(-1, SEC), w_out, x2d, mod3, l,
                                   rowfn_maker, post_mix_g[l])

            x_lat_mix = mix_cd(x_lat, n, _lat_row)
            if use_ctx:
                raise NotImplementedError("odd non-final layers are outside this problem's depth")
        x_lat = _ffn(x_lat_mix, mod3, l, _lat_row, post_ffn_g[l], wg, wu, wd)
        if not last:
            x_ctx = _ffn(x_ctx_mix, mod3, l, _ctx_row, post_ffn_g[l], wg, wu, wd)
    return x_lat.reshape(batch, n, d)
```

```python
import functools
import math

import numpy as np
import jax
import jax.numpy as jnp
from jax import lax
from jax.experimental import pallas as pl
from jax.experimental.pallas import tpu as pltpu

F32 = jnp.float32
BF16 = jnp.bfloat16

D_MODEL = 2048
SEQ = 4096
GRID_W = 64
CTX_LEN = 256
EPS = 1e-6
N_MOD = 6
D_FF = 5632
HEADS = 8
QK_DIM = 64
HEAD_W = 128
SEC = 1024
CHUNK = 128
CONV_W = 31
CONV_PAD = (CONV_W - 1) // 2
ROPE_HALF = 32
ROPE_FREQS = 16
ROPE_BASE = 10000.0
FFT_R = 64
N_COND = 8
CTX_ROW = 4

LANE = 128
SUBLANE = 8
VMEM_LIMIT = 56 * 1024 * 1024
TM = 512
TF = 512
TQ = 256
CK = 512
CONV_T = 512
CONV_RB = 64
HALO = 16
CAST_SPLIT = 2
FFT_NB = 8


def _cparams(*sem):
    return pltpu.CompilerParams(dimension_semantics=sem, vmem_limit_bytes=VMEM_LIMIT)


def _resident():
    return pl.BlockSpec(memory_space=pltpu.VMEM)


def _mod_spec(layer, which, rowfn):
    return pl.BlockSpec((1, 1, D_MODEL),
                        lambda i, *_: ((layer * N_COND + rowfn(i)) * N_MOD + which, 0, 0))


def _lat_row(tm):
    return lambda i: i // (SEQ // tm)


def _ctx_row(_tm):
    return lambda i: CTX_ROW


def _rms(x):
    return x * lax.rsqrt(jnp.mean(x * x, axis=-1, keepdims=True) + EPS)


def _mod_kernel(c_ref, w_ref, b_ref, o_ref):
    s = jax.nn.silu(c_ref[...]).astype(BF16)
    o_ref[0] = jnp.dot(s, w_ref[0].astype(BF16), preferred_element_type=F32) + b_ref[0]


def _modulation(cond, mod_w, mod_b):
    depth, _, n = mod_w.shape
    tn = 1024
    return pl.pallas_call(
        _mod_kernel,
        grid=(depth, n // tn),
        in_specs=[pl.BlockSpec((N_COND, D_MODEL), lambda l, j: (0, 0)),
                  pl.BlockSpec((1, D_MODEL, tn), lambda l, j: (l, 0, j)),
                  pl.BlockSpec((1, 1, tn), lambda l, j: (l, 0, j))],
        out_specs=pl.BlockSpec((1, N_COND, tn), lambda l, j: (l, 0, j)),
        out_shape=jax.ShapeDtypeStruct((depth, N_COND, n), F32),
        compiler_params=_cparams("parallel", "parallel"),
        name="adaln_mod",
    )(cond, mod_w, mod_b.reshape(depth, 1, n))


def _inproj_ab_kernel(*refs, rope):
    if rope:
        (x_ref, sh_ref, sc_ref, w_ref, vg_ref, vb_ref, cos_ref, s1_ref, s2_ref,
         q_ref, k_ref, v_ref, u_ref, vn_ref) = refs
    else:
        x_ref, sh_ref, sc_ref, w_ref, vg_ref, vb_ref, q_ref, k_ref, v_ref, u_ref, vn_ref = refs
    h = _rms(x_ref[...]) * (1.0 + sc_ref[0]) + sh_ref[0]
    hb = h.astype(BF16)

    def section(s):
        return jnp.dot(hb, w_ref[:, s * SEC:(s + 1) * SEC], preferred_element_type=F32)

    def rotate(z):
        if not rope:
            return z
        cos, s1, s2 = cos_ref[...], s1_ref[...], s2_ref[...]
        outs = []
        for hd in range(HEADS):
            zs = z[:, hd * HEAD_W:(hd + 1) * HEAD_W]
            outs.append(zs * cos + pltpu.roll(zs, HEAD_W - ROPE_FREQS, 1) * s1
                        + pltpu.roll(zs, ROPE_FREQS, 1) * s2)
        return jnp.concatenate(outs, axis=1)

    g = jax.nn.gelu(section(4))
    outs = []
    for gi in range(SEC // LANE):
        sl = slice(gi * LANE, (gi + 1) * LANE)
        gs = g[:, sl]
        d = gs - jnp.mean(gs, axis=-1, keepdims=True)
        var = jnp.mean(d * d, axis=-1, keepdims=True)
        outs.append(d * lax.rsqrt(var + EPS) * vg_ref[:, sl] + vb_ref[:, sl])
    vn_ref[...] = jnp.concatenate(outs, axis=1).astype(BF16)
    u_ref[...] = jax.nn.gelu(section(3)).astype(BF16)
    q_ref[...] = (rotate(section(0)) * (QK_DIM ** -0.5)).astype(BF16)
    k_ref[...] = rotate(section(1)).astype(BF16)
    v_ref[...] = section(2).astype(BF16)


def _inproj_ab(x2d, mod3, layer, rowfn_maker, w_in, vnorm_g, vnorm_b, rope_tabs):
    rows = x2d.shape[0]
    tm = TM
    rowfn = rowfn_maker(tm)
    rope = rope_tabs is not None
    row_spec = pl.BlockSpec((tm, D_MODEL), lambda i: (i, 0))
    sec_spec = pl.BlockSpec((tm, SEC), lambda i: (i, 0))
    vec_spec = pl.BlockSpec((1, SEC), lambda i: (0, 0))
    in_specs = [row_spec, _mod_spec(layer, 0, rowfn), _mod_spec(layer, 1, rowfn), _resident(),
                vec_spec, vec_spec]
    args = [x2d, mod3, mod3, w_in, vnorm_g.reshape(1, SEC), vnorm_b.reshape(1, SEC)]
    if rope:
        tab_spec = pl.BlockSpec((tm, HEAD_W), lambda i: (i % (SEQ // tm), 0))
        in_specs += [tab_spec] * 3
        args += list(rope_tabs)
    out = jax.ShapeDtypeStruct((rows, SEC), BF16)
    return pl.pallas_call(
        functools.partial(_inproj_ab_kernel, rope=rope),
        grid=(rows // tm,),
        in_specs=in_specs,
        out_specs=[sec_spec] * 5,
        out_shape=[out] * 5,
        compiler_params=_cparams("parallel"),
        name="inproj_ab_lat" if rope else "inproj_ab_ctx",
    )(*args)


def _attn_kernel(*refs, seg_lens, tq, lam_init, n_cast, cast_period):
    nseg = len(seg_lens)
    q_ref = refs[0]
    k_refs = refs[1:1 + nseg]
    v_refs = refs[1 + nseg:1 + 2 * nseg]
    lam_ref, g_ref = refs[1 + 2 * nseg:3 + 2 * nseg]
    cast_in = refs[3 + 2 * nseg:3 + 2 * nseg + n_cast]
    o_ref = refs[3 + 2 * nseg + n_cast]
    cast_out = refs[4 + 2 * nseg + n_cast:4 + 2 * nseg + 2 * n_cast]
    vt_scr, s_even, s_odd, m_scr = refs[4 + 2 * nseg + 2 * n_cast:]

    if n_cast:
        @pl.when(pl.program_id(2) % cast_period == 0)
        def _():
            for src, dst in zip(cast_in, cast_out):
                dst[...] = src[...].astype(BF16)

    i = pl.program_id(2)
    chunks = []
    base = 0
    for seg, n_keys in enumerate(seg_lens):
        for c0 in range(0, n_keys, CK):
            chunks.append((seg, c0, base + c0, min(CK, n_keys - c0)))
        base += n_keys

    def fold(x, op):
        return op(x.reshape(x.shape[0] // 8, 8, x.shape[1]), axis=0)

    def step(s_write, s_read):
        q_t = q_ref[0].astype(F32).T
        sub = lax.broadcasted_iota(jnp.int32, (HEAD_W, tq), 0)
        zero = jnp.zeros_like(q_t)
        qq_t = jnp.concatenate([jnp.where(sub < QK_DIM, q_t, zero),
                                jnp.where(sub >= QK_DIM, q_t, zero)], axis=1).astype(BF16)
        if s_read is not None:
            m = m_scr[0:1, :]
            lsum = jnp.zeros((8, 2 * tq), F32)
            acc = jnp.zeros((HEAD_W, 2 * tq), F32)
        mx = None
        for seg, c0, r0, w in chunks:
            s = jnp.dot(k_refs[seg][0, c0:c0 + w, :], qq_t, preferred_element_type=F32)
            s_write[r0:r0 + w, :] = s
            t = fold(s, jnp.max)
            mx = t if mx is None else jnp.maximum(mx, t)
            if s_read is not None:
                p = jnp.exp(s_read[r0:r0 + w, :] - m)
                lsum = lsum + fold(p, jnp.sum)
                acc = acc + jnp.dot(vt_scr[:, r0:r0 + w], p.astype(BF16),
                                    preferred_element_type=F32)
        if s_read is not None:
            o_all = acc / jnp.sum(lsum, axis=0, keepdims=True)
            lam_t = lam_ref[...]
            e1 = jnp.exp(jnp.sum(lam_t[0:1, :] * lam_t[1:2, :], axis=-1, keepdims=True))
            e2 = jnp.exp(jnp.sum(lam_t[2:3, :] * lam_t[3:4, :], axis=-1, keepdims=True))
            lam = e1 - e2 + lam_init
            o_t = o_all[:, :tq] - lam * o_all[:, tq:]
            o_t = o_t * lax.rsqrt(jnp.mean(o_t * o_t, axis=0, keepdims=True) + EPS)
            o_t = o_t * g_ref[...] * (1.0 - lam_init)
            o_ref[0] = o_t.T.astype(BF16)
        m_scr[...] = jnp.broadcast_to(jnp.max(mx, axis=0, keepdims=True), m_scr.shape)

    @pl.when(i == 0)
    def _():
        for seg, c0, r0, w in chunks:
            vt_scr[:, r0:r0 + w] = v_refs[seg][0, c0:c0 + w, :].astype(F32).T.astype(BF16)
        step(s_even, None)

    @pl.when(jnp.logical_and(i > 0, i % 2 == 1))
    def _():
        step(s_odd, s_even)

    @pl.when(jnp.logical_and(i > 0, i % 2 == 0))
    def _():
        step(s_even, s_odd)


def _diff_attention(q, ks, vs, lam_tab, subln_g, lam_init, to_cast=()):
    batch, nq, _ = q.shape
    tq = min(TQ, nq)
    nblk = nq // tq
    steps = nblk + 1
    seg_lens = tuple(int(k.shape[1]) for k in ks)
    n_keys = sum(seg_lens)
    q_spec = pl.BlockSpec((1, tq, HEAD_W), lambda b, h, i: (b, jnp.minimum(i, nblk - 1), h))
    o_spec = pl.BlockSpec((1, tq, HEAD_W), lambda b, h, i: (b, jnp.maximum(i - 1, 0), h))
    kv_specs = [pl.BlockSpec((1, n, HEAD_W), lambda b, h, i: (b, 0, h)) for n in seg_lens]
    s_shape = pltpu.VMEM((n_keys, 2 * tq), F32)
    cast_period = -(-steps // CAST_SPLIT)
    n_cast_blocks = batch * HEADS * CAST_SPLIT
    cast_specs = []
    for w in to_cast:
        rows, cols = w.shape
        assert rows % (n_cast_blocks * 2 * SUBLANE) == 0, (rows, n_cast_blocks)
        cast_specs.append(pl.BlockSpec(
            (rows // n_cast_blocks, cols),
            lambda b, h, i: ((b * HEADS + h) * CAST_SPLIT + i // cast_period, 0)))
    outs = pl.pallas_call(
        functools.partial(_attn_kernel, seg_lens=seg_lens, tq=tq, lam_init=lam_init,
                          n_cast=len(to_cast), cast_period=cast_period),
        grid=(batch, HEADS, steps),
        in_specs=[q_spec] + kv_specs + kv_specs
                 + [pl.BlockSpec((N_COND, LANE), lambda b, h, i: (0, 0)),
                    pl.BlockSpec((HEAD_W, 1), lambda b, h, i: (0, 0))] + cast_specs,
        out_specs=[o_spec] + cast_specs,
        out_shape=[jax.ShapeDtypeStruct(q.shape, BF16)]
                  + [jax.ShapeDtypeStruct(w.shape, BF16) for w in to_cast],
        scratch_shapes=[pltpu.VMEM((HEAD_W, n_keys), BF16), s_shape, s_shape,
                        pltpu.VMEM((8, 2 * tq), F32)],
        compiler_params=_cparams("parallel", "parallel", "arbitrary"),
        name="diff_attn_%d" % nq,
    )(q, *ks, *vs, lam_tab, subln_g.reshape(HEAD_W, 1), *to_cast)
    return outs[0], tuple(outs[1:])


def _mix_epilogue(y, x_ref, gate_ref, pg_ref, o_ref):
    o_ref[...] = x_ref[...] + gate_ref[0] * (_rms(y) * pg_ref[...])


def _outproj_ab_kernel(a_ref, u_ref, vn_ref, wsp_ref, bsp_ref, wo_ref, x_ref, gate_ref, pg_ref,
                       o_ref, s_scr, *, tm):
    y = jnp.dot(a_ref[...], wo_ref[0:SEC, :], preferred_element_type=F32)
    for c in range(tm // CHUNK):
        rs = slice(c * CHUNK, (c + 1) * CHUNK)
        for g in range(SEC // LANE):
            cs = slice(g * LANE, (g + 1) * LANE)
            sv = jnp.dot(wsp_ref[g], vn_ref[rs, cs], preferred_element_type=F32) + bsp_ref[g]
            s_scr[rs, cs] = (u_ref[rs, cs].astype(F32) * sv).astype(BF16)
    y = y + jnp.dot(s_scr[...], wo_ref[SEC:2 * SEC, :], preferred_element_type=F32)
    _mix_epilogue(y, x_ref, gate_ref, pg_ref, o_ref)


def _outproj_ab(a2d, u2d, vn2d, w_sp, b_sp, w_out, x2d, mod3, layer, rowfn_maker, post_g):
    rows = x2d.shape[0]
    tm = TM
    rowfn = rowfn_maker(tm)
    sec_spec = pl.BlockSpec((tm, SEC), lambda i: (i, 0))
    row_spec = pl.BlockSpec((tm, D_MODEL), lambda i: (i, 0))
    return pl.pallas_call(
        functools.partial(_outproj_ab_kernel, tm=tm),
        grid=(rows // tm,),
        in_specs=[sec_spec, sec_spec, sec_spec, _resident(), _resident(), _resident(), row_spec,
                  _mod_spec(layer, 2, rowfn), pl.BlockSpec((1, D_MODEL), lambda i: (0, 0))],
        out_specs=row_spec,
        out_shape=jax.ShapeDtypeStruct((rows, D_MODEL), F32),
        scratch_shapes=[pltpu.VMEM((tm, SEC), BF16)],
        compiler_params=_cparams("parallel"),
        name="outproj_ab_%d" % rows,
    )(a2d, u2d, vn2d, w_sp, b_sp, w_out, x2d, mod3, post_g.reshape(1, D_MODEL))


def _outproj_cd_kernel(yc_ref, yd_ref, wo_ref, x_ref, gate_ref, pg_ref, o_ref):
    y = (jnp.dot(yc_ref[...], wo_ref[0:SEC, :], preferred_element_type=F32)
         + jnp.dot(yd_ref[...], wo_ref[SEC:2 * SEC, :], preferred_element_type=F32))
    _mix_epilogue(y, x_ref, gate_ref, pg_ref, o_ref)


def _outproj_cd(yc2d, yd2d, w_out, x2d, mod3, layer, rowfn_maker, post_g):
    rows = x2d.shape[0]
    tm = TM
    rowfn = rowfn_maker(tm)
    sec_spec = pl.BlockSpec((tm, SEC), lambda i: (i, 0))
    row_spec = pl.BlockSpec((tm, D_MODEL), lambda i: (i, 0))
    return pl.pallas_call(
        _outproj_cd_kernel,
        grid=(rows // tm,),
        in_specs=[sec_spec, sec_spec, _resident(), row_spec, _mod_spec(layer, 2, rowfn),
                  pl.BlockSpec((1, D_MODEL), lambda i: (0, 0))],
        out_specs=row_spec,
        out_shape=jax.ShapeDtypeStruct((rows, D_MODEL), F32),
        compiler_params=_cparams("parallel"),
        name="outproj_cd",
    )(yc2d, yd2d, w_out, x2d, mod3, post_g.reshape(1, D_MODEL))


def _ffn_kernel(x_ref, sh_ref, sc_ref, gate_ref, pg_ref, wg_ref, wu_ref, wd_ref, o_ref,
                hb_scr, acc_scr):
    j = pl.program_id(1)
    last = pl.num_programs(1) - 1

    def chunk():
        hb = hb_scr[...]
        g = jnp.dot(hb, wg_ref[0], preferred_element_type=F32)
        u = jnp.dot(hb, wu_ref[0], preferred_element_type=F32)
        a = (jax.nn.silu(g) * u).astype(BF16)
        return jnp.dot(a, wd_ref[0], preferred_element_type=F32)

    @pl.when(j == 0)
    def _():
        h = _rms(x_ref[...]) * (1.0 + sc_ref[0]) + sh_ref[0]
        hb_scr[...] = h.astype(BF16)
        acc_scr[...] = chunk()

    @pl.when(jnp.logical_and(j > 0, j < last))
    def _():
        acc_scr[...] += chunk()

    @pl.when(j == last)
    def _():
        _mix_epilogue(acc_scr[...] + chunk(), x_ref, gate_ref, pg_ref, o_ref)


def _ffn(x2d, mod3, layer, rowfn_maker, post_g, w_gate, w_up, w_down):
    rows = x2d.shape[0]
    tm = TM
    rowfn = rowfn_maker(tm)
    row_spec = pl.BlockSpec((tm, D_MODEL), lambda i, j: (i, 0))
    return pl.pallas_call(
        _ffn_kernel,
        grid=(rows // tm, D_FF // TF),
        in_specs=[row_spec, _mod_spec(layer, 3, rowfn), _mod_spec(layer, 4, rowfn),
                  _mod_spec(layer, 5, rowfn), pl.BlockSpec((1, D_MODEL), lambda i, j: (0, 0)),
                  pl.BlockSpec((1, D_MODEL, TF), lambda i, j: (layer, 0, j)),
                  pl.BlockSpec((1, D_MODEL, TF), lambda i, j: (layer, 0, j)),
                  pl.BlockSpec((1, TF, D_MODEL), lambda i, j: (layer, j, 0))],
        out_specs=row_spec,
        out_shape=jax.ShapeDtypeStruct((rows, D_MODEL), F32),
        scratch_shapes=[pltpu.VMEM((tm, D_MODEL), BF16), pltpu.VMEM((tm, D_MODEL), F32)],
        compiler_params=_cparams("parallel", "arbitrary"),
        name="ffn_%d" % rows,
    )(x2d, mod3, mod3, mod3, post_g.reshape(1, D_MODEL), w_gate, w_up, w_down)


def _inproj_cd_kernel(x_ref, sh_ref, sc_ref, w_ref, y_ref, f_ref):
    h = _rms(x_ref[...]) * (1.0 + sc_ref[0]) + sh_ref[0]
    hb = h.astype(BF16)

    def section(s):
        return jnp.dot(hb, w_ref[:, s * SEC:(s + 1) * SEC], preferred_element_type=F32)

    y_ref[...] = section(0) * jax.nn.sigmoid(section(1))
    f_ref[...] = section(2).astype(BF16)


def _inproj_cd(x2d, mod3, layer, rowfn_maker, w_in):
    rows = x2d.shape[0]
    tm = TM
    rowfn = rowfn_maker(tm)
    sec_spec = pl.BlockSpec((tm, SEC), lambda i: (i, 0))
    return pl.pallas_call(
        _inproj_cd_kernel,
        grid=(rows // tm,),
        in_specs=[pl.BlockSpec((tm, D_MODEL), lambda i: (i, 0)), _mod_spec(layer, 0, rowfn),
                  _mod_spec(layer, 1, rowfn), _resident()],
        out_specs=[sec_spec, sec_spec],
        out_shape=[jax.ShapeDtypeStruct((rows, SEC), F32), jax.ShapeDtypeStruct((rows, SEC), BF16)],
        compiler_params=_cparams("parallel"),
        name="inproj_cd",
    )(x2d, mod3, mod3, w_in)


def _conv_kernel(y_ref, prev_ref, next_ref, w_ref, b_ref, g_ref, be_ref, o_ref, ybuf, cbuf):
    i = pl.program_id(1)
    t = y_ref.shape[1]
    zeros = jnp.zeros((HALO, SEC), F32)
    ybuf[0:HALO, :] = jnp.where(i > 0, prev_ref[0], zeros)
    ybuf[HALO:HALO + t, :] = y_ref[0]
    ybuf[HALO + t:HALO + t + HALO, :] = jnp.where(i < pl.num_programs(1) - 1, next_ref[0], zeros)
    off = HALO - CONV_PAD

    def conv_block(rb, carry):
        r0 = pl.multiple_of(rb * CONV_RB, CONV_RB)
        for c in range(SEC // LANE):
            cs = slice(c * LANE, (c + 1) * LANE)
            win = ybuf[pl.ds(r0, CONV_RB + 2 * HALO), cs]
            acc = jnp.zeros((CONV_RB, LANE), F32) + b_ref[:, cs]
            n_win = CONV_RB + 2 * HALO
            for r in range(SUBLANE):
                shifted = win if r == 0 else pltpu.roll(win, n_win - r, 0)
                for a in range((2 * HALO) // SUBLANE):
                    w = a * SUBLANE + r - off
                    if 0 <= w < CONV_W:
                        acc = acc + (shifted[a * SUBLANE:a * SUBLANE + CONV_RB, :]
                                     * w_ref[w:w + 1, cs])
            cbuf[pl.ds(r0, CONV_RB), cs] = acc
        return carry

    lax.fori_loop(0, t // CONV_RB, conv_block, 0)

    def norm_block(rb, carry):
        r0 = pl.multiple_of(rb * CONV_RB, CONV_RB)
        v = cbuf[pl.ds(r0, CONV_RB), :]
        d = v - jnp.mean(v, axis=-1, keepdims=True)
        var = jnp.mean(d * d, axis=-1, keepdims=True)
        z = d * lax.rsqrt(var + EPS) * g_ref[...] + be_ref[...]
        o_ref[0, pl.ds(r0, CONV_RB), :] = jax.nn.silu(z).astype(BF16)
        return carry

    lax.fori_loop(0, t // CONV_RB, norm_block, 0)


def _conformer_conv(y3d, dw_w, dw_b, norm_g, norm_b):
    batch, n, _ = y3d.shape
    t = CONV_T
    hb = t // HALO
    nh = n // HALO
    vec = pl.BlockSpec((1, SEC), lambda b, i: (0, 0))
    return pl.pallas_call(
        _conv_kernel,
        grid=(batch, n // t),
        in_specs=[pl.BlockSpec((1, t, SEC), lambda b, i: (b, i, 0)),
                  pl.BlockSpec((1, HALO, SEC), lambda b, i: (b, jnp.maximum(i * hb - 1, 0), 0)),
                  pl.BlockSpec((1, HALO, SEC),
                               lambda b, i: (b, jnp.minimum((i + 1) * hb, nh - 1), 0)),
                  pl.BlockSpec((CONV_W, SEC), lambda b, i: (0, 0)), vec, vec, vec],
        out_specs=pl.BlockSpec((1, t, SEC), lambda b, i: (b, i, 0)),
        out_shape=jax.ShapeDtypeStruct((batch, n, SEC), BF16),
        scratch_shapes=[pltpu.VMEM((t + 2 * HALO, SEC), F32), pltpu.VMEM((t, SEC), F32)],
        compiler_params=_cparams("parallel", "arbitrary"),
        name="conformer_conv",
    )(y3d, y3d, y3d, dw_w, dw_b.reshape(1, SEC), norm_g.reshape(1, SEC), norm_b.reshape(1, SEC))


def _dft_constants():
    c = np.arange(LANE)
    ang_c = 2.0 * np.pi * np.outer(c, c) / LANE
    norm = 1.0 / math.sqrt(SEQ * LANE)
    chan = np.concatenate([np.cos(ang_c), -np.sin(ang_c)], axis=1) * norm
    r = np.arange(FFT_R)
    ang_r = 2.0 * np.pi * np.outer(r, r) / FFT_R
    cr, ci = np.cos(ang_r), -np.sin(ang_r)
    stage1 = np.block([[cr, -ci], [ci, cr]])
    stage2 = np.concatenate([cr, -ci], axis=1)
    ang_t = 2.0 * np.pi * np.outer(r, r) / SEQ
    return (jnp.asarray(chan, F32).astype(BF16), jnp.asarray(stage1, F32).astype(BF16),
            jnp.asarray(stage2, F32).astype(BF16),
            jnp.asarray(np.cos(ang_t), F32), jnp.asarray(-np.sin(ang_t), F32))


def _fft_a_kernel(x_ref, chan_ref, m1_ref, twc_ref, tws_ref, ar_ref, ai_ref):
    x = x_ref[0]
    nslab = x.shape[1] // LANE
    xs = jnp.concatenate([x[:, s * LANE:(s + 1) * LANE] for s in range(nslab)], axis=0)
    z = jnp.dot(xs, chan_ref[...], preferred_element_type=F32).astype(BF16)
    zr = jnp.concatenate([z[s * FFT_R:(s + 1) * FFT_R, 0:LANE] for s in range(nslab)], axis=1)
    zi = jnp.concatenate([z[s * FFT_R:(s + 1) * FFT_R, LANE:2 * LANE] for s in range(nslab)],
                         axis=1)
    a = jnp.dot(m1_ref[...], jnp.concatenate([zr, zi], axis=0), preferred_element_type=F32)
    a_r, a_i = a[0:FFT_R], a[FFT_R:2 * FFT_R]
    for j in range(x.shape[1] // SEC):
        cs = slice(j * SEC, (j + 1) * SEC)
        tc = jnp.concatenate([twc_ref[j]] * (SEC // LANE), axis=1)
        ts = jnp.concatenate([tws_ref[j]] * (SEC // LANE), axis=1)
        ar_ref[0, j] = (a_r[:, cs] * tc - a_i[:, cs] * ts).astype(BF16)
        ai_ref[0, j] = (a_r[:, cs] * ts + a_i[:, cs] * tc).astype(BF16)


def _fft_c_kernel(ar_ref, ai_ref, m2_ref, o_ref):
    a = jnp.concatenate([ar_ref[0], ai_ref[0]], axis=0)
    o_ref[0] = jnp.dot(m2_ref[...], a, preferred_element_type=F32).astype(BF16)


def _fourier_mix(f3d):
    batch, n, _ = f3d.shape
    chan, m1, m2, twc, tws = _dft_constants()
    twc = jnp.broadcast_to(twc[:, :, None], (FFT_R, FFT_R, LANE))
    tws = jnp.broadcast_to(tws[:, :, None], (FFT_R, FFT_R, LANE))
    nb = FFT_NB
    wide = nb * SEC
    xv = f3d.reshape(batch, FFT_R, FFT_R * SEC)
    blk = pl.BlockSpec((1, FFT_R, wide), lambda b, j: (b, 0, j))
    a_spec = pl.BlockSpec((1, nb, FFT_R, SEC), lambda b, j: (b, j, 0, 0))
    tw_spec = pl.BlockSpec((nb, FFT_R, LANE), lambda b, j: (j, 0, 0))
    a_shape = jax.ShapeDtypeStruct((batch, FFT_R, FFT_R, SEC), BF16)
    a_r, a_i = pl.pallas_call(
        _fft_a_kernel,
        grid=(batch, FFT_R // nb),
        in_specs=[blk, pl.BlockSpec((LANE, 2 * LANE), lambda b, j: (0, 0)),
                  pl.BlockSpec((2 * FFT_R, 2 * FFT_R), lambda b, j: (0, 0)), tw_spec, tw_spec],
        out_specs=[a_spec, a_spec],
        out_shape=[a_shape, a_shape],
        compiler_params=_cparams("parallel", "parallel"),
        name="fourier_stage_a",
    )(xv, chan, m1, twc, tws)
    out = pl.pallas_call(
        _fft_c_kernel,
        grid=(batch, FFT_R // nb),
        in_specs=[blk, blk, pl.BlockSpec((FFT_R, 2 * FFT_R), lambda b, j: (0, 0))],
        out_specs=blk,
        out_shape=jax.ShapeDtypeStruct((batch, FFT_R, FFT_R * SEC), BF16),
        compiler_params=_cparams("parallel", "parallel"),
        name="fourier_stage_c",
    )(a_r.reshape(batch, FFT_R, FFT_R * SEC), a_i.reshape(batch, FFT_R, FFT_R * SEC), m2)
    return out.reshape(batch, n, SEC)


def _rope_tables(n):
    rows = jnp.repeat(jnp.arange(n // GRID_W, dtype=F32), GRID_W)
    cols = jnp.tile(jnp.arange(GRID_W, dtype=F32), n // GRID_W)
    inv = ROPE_BASE ** (-jnp.arange(ROPE_FREQS, dtype=F32) / ROPE_FREQS)
    lane = np.arange(HEAD_W)
    freq = lane % ROPE_FREQS
    by_col = (lane % QK_DIM) // ROPE_HALF == 1
    second = (lane % ROPE_HALF) // ROPE_FREQS == 1
    ang = jnp.where(by_col[None, :], (cols[:, None] * inv)[:, freq], (rows[:, None] * inv)[:, freq])
    cos, sin = jnp.cos(ang), jnp.sin(ang)
    zero = jnp.zeros_like(sin)
    return cos, jnp.where(second[None, :], zero, -sin), jnp.where(second[None, :], sin, zero)


def kernel(x, c, ctx, c_ctx, mod_w, mod_b, post_mix_g, post_ffn_g, ffn_w_gate, ffn_w_up, ffn_w_down, ab_w_in, ab_w_out, ab_lam_q1, ab_lam_k1, ab_lam_q2, ab_lam_k2, ab_subln_g, ab_vnorm_g, ab_vnorm_b, ab_w_spatial, ab_b_spatial, cd_w_in, cd_w_out, cd_dw_w, cd_dw_b, cd_norm_g, cd_norm_b):
    batch, n, d = x.shape
    m = ctx.shape[1]
    depth = mod_w.shape[0]
    assert (n, d, m) == (SEQ, D_MODEL, CTX_LEN) and batch <= CTX_ROW

    cond = jnp.concatenate([c, c_ctx[None, :], jnp.zeros((N_COND - batch - 1, d), F32)], axis=0)
    mod3 = _modulation(cond, mod_w, mod_b).reshape(depth * N_COND * N_MOD, 1, d)
    rope_tabs = _rope_tables(n)
    stacks = [ffn_w_gate, ffn_w_up, ffn_w_down, ab_w_out, cd_w_in, cd_w_out]
    cast_later = [w.reshape(-1, w.shape[-1]) for w in stacks]

    x_lat = x.reshape(batch * n, d)
    x_ctx = ctx.reshape(batch * m, d)
    for l in range(depth):
        last = l == depth - 1
        even = l % 2 == 0
        i = l // 2
        use_ctx = (not last) or even
        if even:
            lam_init = 0.8 - 0.6 * math.exp(-0.3 * l)
            assert l == 0, "the weight-cast side job is attached to the first layer's attention"
            w_in = ab_w_in[i].astype(BF16)
            w_sp = ab_w_spatial[i].astype(BF16)
            b_sp = ab_b_spatial[i].reshape(SEC // LANE, CHUNK, 1)
            lam_tab = jnp.zeros((N_COND, LANE), F32).at[0:4, 0:QK_DIM].set(
                jnp.stack([ab_lam_q1[i], ab_lam_k1[i], ab_lam_q2[i], ab_lam_k2[i]]))
            q, k, v, u, vn = _inproj_ab(x_lat, mod3, l, _lat_row, w_in, ab_vnorm_g[i],
                                        ab_vnorm_b[i], rope_tabs)
            qc, kc, vc, uc, vnc = _inproj_ab(x_ctx, mod3, l, _ctx_row, w_in, ab_vnorm_g[i],
                                             ab_vnorm_b[i], None)
            q3, k3, v3 = (t.reshape(batch, n, SEC) for t in (q, k, v))
            kc3, vc3 = kc.reshape(batch, m, SEC), vc.reshape(batch, m, SEC)
            a_lat, cast = _diff_attention(q3, [k3, kc3], [v3, vc3], lam_tab, ab_subln_g[i], lam_init,
                                          to_cast=cast_later)
            wg, wu, wd, ab_w_out_b, cd_w_in_b, cd_w_out_b = (
                c2.reshape(w.shape) for c2, w in zip(cast, stacks))
            w_out = ab_w_out_b[i]
            x_lat_mix = _outproj_ab(a_lat.reshape(batch * n, SEC), u, vn, w_sp, b_sp, w_out, x_lat,
                                    mod3, l, _lat_row, post_mix_g[l])
            if not last:
                a_ctx, _ = _diff_attention(qc.reshape(batch, m, SEC), [kc3], [vc3], lam_tab,
                                           ab_subln_g[i], lam_init)
                x_ctx_mix = _outproj_ab(a_ctx.reshape(batch * m, SEC), uc, vnc, w_sp, b_sp, w_out,
                                        x_ctx, mod3, l, _ctx_row, post_mix_g[l])
        else:
            w_in, w_out = cd_w_in_b[i], cd_w_out_b[i]

            def mix_cd(x2d, rows_per_seq, rowfn_maker):
                y, f = _inproj_cd(x2d, mod3, l, rowfn_maker, w_in)
                nb = x2d.shape[0] // rows_per_seq
                yc = _conformer_conv(y.reshape(nb, rows_per_seq, SEC), cd_dw_w[i], cd_dw_b[i],
                                     cd_norm_g[i], cd_norm_b[i])
                yd = _fourier_mix(f.reshape(nb, rows_per_seq, SEC))
                return _outproj_cd(yc.reshape(-1, SEC), yd.reshape(-1, SEC), w_out, x2d, mod3, l,
                                   rowfn_maker, post_mix_g[l])

            x_lat_mix = mix_cd(x_lat, n, _lat_row)
            if use_ctx:
                raise NotImplementedError("odd non-final layers are outside this problem's depth")
        x_lat = _ffn(x_lat_mix, mod3, l, _lat_row, post_ffn_g[l], wg, wu, wd)
        if not last:
            x_ctx = _ffn(x_ctx_mix, mod3, l, _ctx_row, post_ffn_g[l], wg, wu, wd)
    return x_lat.reshape(batch, n, d)
```

```python
import functools
import math

import numpy as np
import jax
import jax.numpy as jnp
from jax import lax
from jax.experimental import pallas as pl
from jax.experimental.pallas import tpu as pltpu

F32 = jnp.float32
BF16 = jnp.bfloat16

D_MODEL = 2048
SEQ = 4096
GRID_W = 64
CTX_LEN = 256
EPS = 1e-6
N_MOD = 6
D_FF = 5632
HEADS = 8
QK_DIM = 64
HEAD_W = 128
SEC = 1024
CHUNK = 128
CONV_W = 31
CONV_PAD = (CONV_W - 1) // 2
ROPE_HALF = 32
ROPE_FREQS = 16
ROPE_BASE = 10000.0
FFT_R = 64
N_COND = 8
CTX_ROW = 4

LANE = 128
SUBLANE = 8
VMEM_LIMIT = 56 * 1024 * 1024
TM = 512
TF = 512
TQ = 256
CK = 512
CONV_T = 512
CONV_RB = 64
HALO = 16
FFT_KB = 16
CAST_SPLIT = 2
FFT_NB = 8


def _cparams(*sem):
    return pltpu.CompilerParams(dimension_semantics=sem, vmem_limit_bytes=VMEM_LIMIT)


def _resident():
    return pl.BlockSpec(memory_space=pltpu.VMEM)


def _mod_spec(layer, which, rowfn):
    return pl.BlockSpec((1, 1, D_MODEL),
                        lambda i, *_: ((layer * N_COND + rowfn(i)) * N_MOD + which, 0, 0))


def _lat_row(tm):
    return lambda i: i // (SEQ // tm)


def _ctx_row(_tm):
    return lambda i: CTX_ROW


def _rms(x):
    return x * lax.rsqrt(jnp.mean(x * x, axis=-1, keepdims=True) + EPS)


def _mod_kernel(c_ref, w_ref, b_ref, o_ref):
    s = jax.nn.silu(c_ref[...]).astype(BF16)
    o_ref[0] = jnp.dot(s, w_ref[0].astype(BF16), preferred_element_type=F32) + b_ref[0]


def _modulation(cond, mod_w, mod_b):
    depth, _, n = mod_w.shape
    tn = 1024
    return pl.pallas_call(
        _mod_kernel,
        grid=(depth, n // tn),
        in_specs=[pl.BlockSpec((N_COND, D_MODEL), lambda l, j: (0, 0)),
                  pl.BlockSpec((1, D_MODEL, tn), lambda l, j: (l, 0, j)),
                  pl.BlockSpec((1, 1, tn), lambda l, j: (l, 0, j))],
        out_specs=pl.BlockSpec((1, N_COND, tn), lambda l, j: (l, 0, j)),
        out_shape=jax.ShapeDtypeStruct((depth, N_COND, n), F32),
        compiler_params=_cparams("parallel", "parallel"),
        name="adaln_mod",
    )(cond, mod_w, mod_b.reshape(depth, 1, n))


def _inproj_ab_kernel(*refs, rope):
    if rope:
        (x_ref, sh_ref, sc_ref, w_ref, vg_ref, vb_ref, cos_ref, s1_ref, s2_ref,
         q_ref, k_ref, v_ref, u_ref, vn_ref) = refs
    else:
        x_ref, sh_ref, sc_ref, w_ref, vg_ref, vb_ref, q_ref, k_ref, v_ref, u_ref, vn_ref = refs
    h = _rms(x_ref[...]) * (1.0 + sc_ref[0]) + sh_ref[0]
    hb = h.astype(BF16)

    def section(s):
        return jnp.dot(hb, w_ref[:, s * SEC:(s + 1) * SEC], preferred_element_type=F32)

    def rotate(z):
        if not rope:
            return z
        cos, s1, s2 = cos_ref[...], s1_ref[...], s2_ref[...]
        outs = []
        for hd in range(HEADS):
            zs = z[:, hd * HEAD_W:(hd + 1) * HEAD_W]
            outs.append(zs * cos + pltpu.roll(zs, HEAD_W - ROPE_FREQS, 1) * s1
                        + pltpu.roll(zs, ROPE_FREQS, 1) * s2)
        return jnp.concatenate(outs, axis=1)

    g = jax.nn.gelu(section(4))
    outs = []
    for gi in range(SEC // LANE):
        sl = slice(gi * LANE, (gi + 1) * LANE)
        gs = g[:, sl]
        d = gs - jnp.mean(gs, axis=-1, keepdims=True)
        var = jnp.mean(d * d, axis=-1, keepdims=True)
        outs.append(d * lax.rsqrt(var + EPS) * vg_ref[:, sl] + vb_ref[:, sl])
    vn_ref[...] = jnp.concatenate(outs, axis=1).astype(BF16)
    u_ref[...] = jax.nn.gelu(section(3)).astype(BF16)
    q_ref[...] = (rotate(section(0)) * (QK_DIM ** -0.5)).astype(BF16)
    k_ref[...] = rotate(section(1)).astype(BF16)
    v_ref[...] = section(2).astype(BF16)


def _inproj_ab(x2d, mod3, layer, rowfn_maker, w_in, vnorm_g, vnorm_b, rope_tabs):
    rows = x2d.shape[0]
    tm = TM
    rowfn = rowfn_maker(tm)
    rope = rope_tabs is not None
    row_spec = pl.BlockSpec((tm, D_MODEL), lambda i: (i, 0))
    sec_spec = pl.BlockSpec((tm, SEC), lambda i: (i, 0))
    vec_spec = pl.BlockSpec((1, SEC), lambda i: (0, 0))
    in_specs = [row_spec, _mod_spec(layer, 0, rowfn), _mod_spec(layer, 1, rowfn), _resident(),
                vec_spec, vec_spec]
    args = [x2d, mod3, mod3, w_in, vnorm_g.reshape(1, SEC), vnorm_b.reshape(1, SEC)]
    if rope:
        tab_spec = pl.BlockSpec((tm, HEAD_W), lambda i: (i % (SEQ // tm), 0))
        in_specs += [tab_spec] * 3
        args += list(rope_tabs)
    out = jax.ShapeDtypeStruct((rows, SEC), BF16)
    return pl.pallas_call(
        functools.partial(_inproj_ab_kernel, rope=rope),
        grid=(rows // tm,),
        in_specs=in_specs,
        out_specs=[sec_spec] * 5,
        out_shape=[out] * 5,
        compiler_params=_cparams("parallel"),
        name="inproj_ab_lat" if rope else "inproj_ab_ctx",
    )(*args)


def _attn_kernel(*refs, seg_lens, tq, lam_init, n_cast, cast_period):
    nseg = len(seg_lens)
    q_ref = refs[0]
    k_refs = refs[1:1 + nseg]
    v_refs = refs[1 + nseg:1 + 2 * nseg]
    lam_ref, g_ref = refs[1 + 2 * nseg:3 + 2 * nseg]
    cast_in = refs[3 + 2 * nseg:3 + 2 * nseg + n_cast]
    o_ref = refs[3 + 2 * nseg + n_cast]
    cast_out = refs[4 + 2 * nseg + n_cast:4 + 2 * nseg + 2 * n_cast]
    vt_scr, s_even, s_odd, m_scr = refs[4 + 2 * nseg + 2 * n_cast:]

    if n_cast:
        @pl.when(pl.program_id(2) % cast_period == 0)
        def _():
            for src, dst in zip(cast_in, cast_out):
                dst[...] = src[...].astype(BF16)

    i = pl.program_id(2)
    chunks = []
    base = 0
    for seg, n_keys in enumerate(seg_lens):
        for c0 in range(0, n_keys, CK):
            chunks.append((seg, c0, base + c0, min(CK, n_keys - c0)))
        base += n_keys

    def fold(x, op):
        return op(x.reshape(x.shape[0] // 8, 8, x.shape[1]), axis=0)

    def step(s_write, s_read):
        q_t = q_ref[0].astype(F32).T
        sub = lax.broadcasted_iota(jnp.int32, (HEAD_W, tq), 0)
        zero = jnp.zeros_like(q_t)
        qq_t = jnp.concatenate([jnp.where(sub < QK_DIM, q_t, zero),
                                jnp.where(sub >= QK_DIM, q_t, zero)], axis=1).astype(BF16)
        if s_read is not None:
            m = m_scr[0:1, :]
            lsum = jnp.zeros((8, 2 * tq), F32)
            acc = jnp.zeros((HEAD_W, 2 * tq), F32)
        mx = None
        for seg, c0, r0, w in chunks:
            s = jnp.dot(k_refs[seg][0, c0:c0 + w, :], qq_t, preferred_element_type=F32)
            s_write[r0:r0 + w, :] = s
            t = fold(s, jnp.max)
            mx = t if mx is None else jnp.maximum(mx, t)
            if s_read is not None:
                p = jnp.exp(s_read[r0:r0 + w, :] - m)
                lsum = lsum + fold(p, jnp.sum)
                acc = acc + jnp.dot(vt_scr[:, r0:r0 + w], p.astype(BF16),
                                    preferred_element_type=F32)
        if s_read is not None:
            o_all = acc / jnp.sum(lsum, axis=0, keepdims=True)
            lam_t = lam_ref[...]
            e1 = jnp.exp(jnp.sum(lam_t[0:1, :] * lam_t[1:2, :], axis=-1, keepdims=True))
            e2 = jnp.exp(jnp.sum(lam_t[2:3, :] * lam_t[3:4, :], axis=-1, keepdims=True))
            lam = e1 - e2 + lam_init
            o_t = o_all[:, :tq] - lam * o_all[:, tq:]
            o_t = o_t * lax.rsqrt(jnp.mean(o_t * o_t, axis=0, keepdims=True) + EPS)
            o_t = o_t * g_ref[...] * (1.0 - lam_init)
            o_ref[0] = o_t.T.astype(BF16)
        m_scr[...] = jnp.broadcast_to(jnp.max(mx, axis=0, keepdims=True), m_scr.shape)

    @pl.when(i == 0)
    def _():
        for seg, c0, r0, w in chunks:
            vt_scr[:, r0:r0 + w] = v_refs[seg][0, c0:c0 + w, :].astype(F32).T.astype(BF16)
        step(s_even, None)

    @pl.when(jnp.logical_and(i > 0, i % 2 == 1))
    def _():
        step(s_odd, s_even)

    @pl.when(jnp.logical_and(i > 0, i % 2 == 0))
    def _():
        step(s_even, s_odd)


def _diff_attention(q, ks, vs, lam_tab, subln_g, lam_init, to_cast=()):
    batch, nq, _ = q.shape
    tq = min(TQ, nq)
    nblk = nq // tq
    steps = nblk + 1
    seg_lens = tuple(int(k.shape[1]) for k in ks)
    n_keys = sum(seg_lens)
    q_spec = pl.BlockSpec((1, tq, HEAD_W), lambda b, h, i: (b, jnp.minimum(i, nblk - 1), h))
    o_spec = pl.BlockSpec((1, tq, HEAD_W), lambda b, h, i: (b, jnp.maximum(i - 1, 0), h))
    kv_specs = [pl.BlockSpec((1, n, HEAD_W), lambda b, h, i: (b, 0, h)) for n in seg_lens]
    s_shape = pltpu.VMEM((n_keys, 2 * tq), F32)
    cast_period = -(-steps // CAST_SPLIT)
    n_cast_blocks = batch * HEADS * CAST_SPLIT
    cast_specs = []
    for w in to_cast:
        rows, cols = w.shape
        assert rows % (n_cast_blocks * 2 * SUBLANE) == 0, (rows, n_cast_blocks)
        cast_specs.append(pl.BlockSpec(
            (rows // n_cast_blocks, cols),
            lambda b, h, i: ((b * HEADS + h) * CAST_SPLIT + i // cast_period, 0)))
    outs = pl.pallas_call(
        functools.partial(_attn_kernel, seg_lens=seg_lens, tq=tq, lam_init=lam_init,
                          n_cast=len(to_cast), cast_period=cast_period),
        grid=(batch, HEADS, steps),
        in_specs=[q_spec] + kv_specs + kv_specs
                 + [pl.BlockSpec((N_COND, LANE), lambda b, h, i: (0, 0)),
                    pl.BlockSpec((HEAD_W, 1), lambda b, h, i: (0, 0))] + cast_specs,
        out_specs=[o_spec] + cast_specs,
        out_shape=[jax.ShapeDtypeStruct(q.shape, BF16)]
                  + [jax.ShapeDtypeStruct(w.shape, BF16) for w in to_cast],
        scratch_shapes=[pltpu.VMEM((HEAD_W, n_keys), BF16), s_shape, s_shape,
                        pltpu.VMEM((8, 2 * tq), F32)],
        compiler_params=_cparams("parallel", "parallel", "arbitrary"),
        name="diff_attn_%d" % nq,
    )(q, *ks, *vs, lam_tab, subln_g.reshape(HEAD_W, 1), *to_cast)
    return outs[0], tuple(outs[1:])


def _mix_epilogue(y, x_ref, gate_ref, pg_ref, o_ref):
    o_ref[...] = x_ref[...] + gate_ref[0] * (_rms(y) * pg_ref[...])


def _outproj_ab_kernel(a_ref, u_ref, vn_ref, wsp_ref, bsp_ref, wo_ref, x_ref, gate_ref, pg_ref,
                       o_ref, s_scr, *, tm):
    y = jnp.dot(a_ref[...], wo_ref[0:SEC, :], preferred_element_type=F32)
    for c in range(tm // CHUNK):
        rs = slice(c * CHUNK, (c + 1) * CHUNK)
        for g in range(SEC // LANE):
            cs = slice(g * LANE, (g + 1) * LANE)
            sv = jnp.dot(wsp_ref[g], vn_ref[rs, cs], preferred_element_type=F32) + bsp_ref[g]
            s_scr[rs, cs] = (u_ref[rs, cs].astype(F32) * sv).astype(BF16)
    y = y + jnp.dot(s_scr[...], wo_ref[SEC:2 * SEC, :], preferred_element_type=F32)
    _mix_epilogue(y, x_ref, gate_ref, pg_ref, o_ref)


def _outproj_ab(a2d, u2d, vn2d, w_sp, b_sp, w_out, x2d, mod3, layer, rowfn_maker, post_g):
    rows = x2d.shape[0]
    tm = TM
    rowfn = rowfn_maker(tm)
    sec_spec = pl.BlockSpec((tm, SEC), lambda i: (i, 0))
    row_spec = pl.BlockSpec((tm, D_MODEL), lambda i: (i, 0))
    return pl.pallas_call(
        functools.partial(_outproj_ab_kernel, tm=tm),
        grid=(rows // tm,),
        in_specs=[sec_spec, sec_spec, sec_spec, _resident(), _resident(), _resident(), row_spec,
                  _mod_spec(layer, 2, rowfn), pl.BlockSpec((1, D_MODEL), lambda i: (0, 0))],
        out_specs=row_spec,
        out_shape=jax.ShapeDtypeStruct((rows, D_MODEL), F32),
        scratch_shapes=[pltpu.VMEM((tm, SEC), BF16)],
        compiler_params=_cparams("parallel"),
        name="outproj_ab_%d" % rows,
    )(a2d, u2d, vn2d, w_sp, b_sp, w_out, x2d, mod3, post_g.reshape(1, D_MODEL))


def _outproj_cd_kernel(yc_ref, yd_ref, wo_ref, x_ref, gate_ref, pg_ref, o_ref):
    y = (jnp.dot(yc_ref[...], wo_ref[0:SEC, :], preferred_element_type=F32)
         + jnp.dot(yd_ref[...], wo_ref[SEC:2 * SEC, :], preferred_element_type=F32))
    _mix_epilogue(y, x_ref, gate_ref, pg_ref, o_ref)


def _outproj_cd(yc2d, yd2d, w_out, x2d, mod3, layer, rowfn_maker, post_g):
    rows = x2d.shape[0]
    tm = TM
    rowfn = rowfn_maker(tm)
    sec_spec = pl.BlockSpec((tm, SEC), lambda i: (i, 0))
    row_spec = pl.BlockSpec((tm, D_MODEL), lambda i: (i, 0))
    return pl.pallas_call(
        _outproj_cd_kernel,
        grid=(rows // tm,),
        in_specs=[sec_spec, sec_spec, _resident(), row_spec, _mod_spec(layer, 2, rowfn),
                  pl.BlockSpec((1, D_MODEL), lambda i: (0, 0))],
        out_specs=row_spec,
        out_shape=jax.ShapeDtypeStruct((rows, D_MODEL), F32),
        compiler_params=_cparams("parallel"),
        name="outproj_cd",
    )(yc2d, yd2d, w_out, x2d, mod3, post_g.reshape(1, D_MODEL))


def _ffn_kernel(x_ref, sh_ref, sc_ref, gate_ref, pg_ref, wg_ref, wu_ref, wd_ref, o_ref,
                hb_scr, acc_scr):
    j = pl.program_id(1)
    last = pl.num_programs(1) - 1

    def chunk():
        hb = hb_scr[...]
        g = jnp.dot(hb, wg_ref[0], preferred_element_type=F32)
        u = jnp.dot(hb, wu_ref[0], preferred_element_type=F32)
        a = (jax.nn.silu(g) * u).astype(BF16)
        return jnp.dot(a, wd_ref[0], preferred_element_type=F32)

    @pl.when(j == 0)
    def _():
        h = _rms(x_ref[...]) * (1.0 + sc_ref[0]) + sh_ref[0]
        hb_scr[...] = h.astype(BF16)
        acc_scr[...] = chunk()

    @pl.when(jnp.logical_and(j > 0, j < last))
    def _():
        acc_scr[...] += chunk()

    @pl.when(j == last)
    def _():
        _mix_epilogue(acc_scr[...] + chunk(), x_ref, gate_ref, pg_ref, o_ref)


def _ffn(x2d, mod3, layer, rowfn_maker, post_g, w_gate, w_up, w_down):
    rows = x2d.shape[0]
    tm = TM
    rowfn = rowfn_maker(tm)
    row_spec = pl.BlockSpec((tm, D_MODEL), lambda i, j: (i, 0))
    return pl.pallas_call(
        _ffn_kernel,
        grid=(rows // tm, D_FF // TF),
        in_specs=[row_spec, _mod_spec(layer, 3, rowfn), _mod_spec(layer, 4, rowfn),
                  _mod_spec(layer, 5, rowfn), pl.BlockSpec((1, D_MODEL), lambda i, j: (0, 0)),
                  pl.BlockSpec((1, D_MODEL, TF), lambda i, j: (layer, 0, j)),
                  pl.BlockSpec((1, D_MODEL, TF), lambda i, j: (layer, 0, j)),
                  pl.BlockSpec((1, TF, D_MODEL), lambda i, j: (layer, j, 0))],
        out_specs=row_spec,
        out_shape=jax.ShapeDtypeStruct((rows, D_MODEL), F32),
        scratch_shapes=[pltpu.VMEM((tm, D_MODEL), BF16), pltpu.VMEM((tm, D_MODEL), F32)],
        compiler_params=_cparams("parallel", "arbitrary"),
        name="ffn_%d" % rows,
    )(x2d, mod3, mod3, mod3, post_g.reshape(1, D_MODEL), w_gate, w_up, w_down)


def _inproj_cd_kernel(x_ref, sh_ref, sc_ref, w_ref, y_ref, f_ref):
    h = _rms(x_ref[...]) * (1.0 + sc_ref[0]) + sh_ref[0]
    hb = h.astype(BF16)

    def section(s):
        return jnp.dot(hb, w_ref[:, s * SEC:(s + 1) * SEC], preferred_element_type=F32)

    y_ref[...] = section(0) * jax.nn.sigmoid(section(1))
    f_ref[...] = section(2).astype(BF16)


def _inproj_cd(x2d, mod3, layer, rowfn_maker, w_in):
    rows = x2d.shape[0]
    tm = TM
    rowfn = rowfn_maker(tm)
    sec_spec = pl.BlockSpec((tm, SEC), lambda i: (i, 0))
    return pl.pallas_call(
        _inproj_cd_kernel,
        grid=(rows // tm,),
        in_specs=[pl.BlockSpec((tm, D_MODEL), lambda i: (i, 0)), _mod_spec(layer, 0, rowfn),
                  _mod_spec(layer, 1, rowfn), _resident()],
        out_specs=[sec_spec, sec_spec],
        out_shape=[jax.ShapeDtypeStruct((rows, SEC), F32), jax.ShapeDtypeStruct((rows, SEC), BF16)],
        compiler_params=_cparams("parallel"),
        name="inproj_cd",
    )(x2d, mod3, mod3, w_in)


def _conv_kernel(y_ref, prev_ref, next_ref, w_ref, b_ref, g_ref, be_ref, o_ref, ybuf, cbuf):
    i = pl.program_id(1)
    t = y_ref.shape[1]
    zeros = jnp.zeros((HALO, SEC), F32)
    ybuf[0:HALO, :] = jnp.where(i > 0, prev_ref[0], zeros)
    ybuf[HALO:HALO + t, :] = y_ref[0]
    ybuf[HALO + t:HALO + t + HALO, :] = jnp.where(i < pl.num_programs(1) - 1, next_ref[0], zeros)
    off = HALO - CONV_PAD

    def conv_block(rb, carry):
        r0 = pl.multiple_of(rb * CONV_RB, CONV_RB)
        for c in range(SEC // LANE):
            cs = slice(c * LANE, (c + 1) * LANE)
            win = ybuf[pl.ds(r0, CONV_RB + 2 * HALO), cs]
            acc = jnp.zeros((CONV_RB, LANE), F32) + b_ref[:, cs]
            n_win = CONV_RB + 2 * HALO
            for r in range(SUBLANE):
                shifted = win if r == 0 else pltpu.roll(win, n_win - r, 0)
                for a in range((2 * HALO) // SUBLANE):
                    w = a * SUBLANE + r - off
                    if 0 <= w < CONV_W:
                        acc = acc + (shifted[a * SUBLANE:a * SUBLANE + CONV_RB, :]
                                     * w_ref[w:w + 1, cs])
            cbuf[pl.ds(r0, CONV_RB), cs] = acc
        return carry

    lax.fori_loop(0, t // CONV_RB, conv_block, 0)

    def norm_block(rb, carry):
        r0 = pl.multiple_of(rb * CONV_RB, CONV_RB)
        v = cbuf[pl.ds(r0, CONV_RB), :]
        d = v - jnp.mean(v, axis=-1, keepdims=True)
        var = jnp.mean(d * d, axis=-1, keepdims=True)
        z = d * lax.rsqrt(var + EPS) * g_ref[...] + be_ref[...]
        o_ref[0, pl.ds(r0, CONV_RB), :] = jax.nn.silu(z).astype(BF16)
        return carry

    lax.fori_loop(0, t // CONV_RB, norm_block, 0)


def _conformer_conv(y3d, dw_w, dw_b, norm_g, norm_b):
    batch, n, _ = y3d.shape
    t = CONV_T
    hb = t // HALO
    nh = n // HALO
    vec = pl.BlockSpec((1, SEC), lambda b, i: (0, 0))
    return pl.pallas_call(
        _conv_kernel,
        grid=(batch, n // t),
        in_specs=[pl.BlockSpec((1, t, SEC), lambda b, i: (b, i, 0)),
                  pl.BlockSpec((1, HALO, SEC), lambda b, i: (b, jnp.maximum(i * hb - 1, 0), 0)),
                  pl.BlockSpec((1, HALO, SEC),
                               lambda b, i: (b, jnp.minimum((i + 1) * hb, nh - 1), 0)),
                  pl.BlockSpec((CONV_W, SEC), lambda b, i: (0, 0)), vec, vec, vec],
        out_specs=pl.BlockSpec((1, t, SEC), lambda b, i: (b, i, 0)),
        out_shape=jax.ShapeDtypeStruct((batch, n, SEC), BF16),
        scratch_shapes=[pltpu.VMEM((t + 2 * HALO, SEC), F32), pltpu.VMEM((t, SEC), F32)],
        compiler_params=_cparams("parallel", "arbitrary"),
        name="conformer_conv",
    )(y3d, y3d, y3d, dw_w, dw_b.reshape(1, SEC), norm_g.reshape(1, SEC), norm_b.reshape(1, SEC))


def _dft_constants():
    c = np.arange(LANE)
    ang_c = 2.0 * np.pi * np.outer(c, c) / LANE
    norm = 1.0 / math.sqrt(SEQ * LANE)
    chan = np.concatenate([np.cos(ang_c), -np.sin(ang_c)], axis=1) * norm
    r = np.arange(FFT_R)
    ang_r = 2.0 * np.pi * np.outer(r, r) / FFT_R
    cr, ci = np.cos(ang_r), -np.sin(ang_r)
    stage1 = np.block([[cr, -ci], [ci, cr]])
    eye = np.eye(FFT_KB)
    stage2_r, stage2_i = np.kron(cr, eye), np.kron(-ci, eye)
    ang_t = 2.0 * np.pi * np.outer(r, r) / SEQ
    return (jnp.asarray(chan, F32).astype(BF16), jnp.asarray(stage1, F32).astype(BF16),
            jnp.asarray(stage2_r, F32).astype(BF16), jnp.asarray(stage2_i, F32).astype(BF16),
            jnp.asarray(np.cos(ang_t), F32), jnp.asarray(-np.sin(ang_t), F32))


def _fft_a_kernel(x_ref, chan_ref, m1_ref, twc_ref, tws_ref, ar_ref, ai_ref):
    x = x_ref[0]
    nslab = x.shape[1] // LANE
    xs = jnp.concatenate([x[:, s * LANE:(s + 1) * LANE] for s in range(nslab)], axis=0)
    z = jnp.dot(xs, chan_ref[...], preferred_element_type=F32).astype(BF16)
    zr = jnp.concatenate([z[s * FFT_R:(s + 1) * FFT_R, 0:LANE] for s in range(nslab)], axis=1)
    zi = jnp.concatenate([z[s * FFT_R:(s + 1) * FFT_R, LANE:2 * LANE] for s in range(nslab)],
                         axis=1)
    a = jnp.dot(m1_ref[...], jnp.concatenate([zr, zi], axis=0), preferred_element_type=F32)
    a_r, a_i = a[0:FFT_R], a[FFT_R:2 * FFT_R]
    for j in range(x.shape[1] // SEC):
        cs = slice(j * SEC, (j + 1) * SEC)
        tc = jnp.concatenate([twc_ref[j]] * (SEC // LANE), axis=1)
        ts = jnp.concatenate([tws_ref[j]] * (SEC // LANE), axis=1)
        ar_ref[0, j] = (a_r[:, cs] * tc - a_i[:, cs] * ts).astype(BF16)
        ai_ref[0, j] = (a_r[:, cs] * ts + a_i[:, cs] * tc).astype(BF16)


def _fft_c_kernel(ar_ref, ai_ref, mr_ref, mi_ref, o_ref):
    rows = FFT_R * FFT_KB
    x_r = ar_ref[0].reshape(rows, SEC)
    x_i = ai_ref[0].reshape(rows, SEC)
    y = (jnp.dot(mr_ref[...], x_r, preferred_element_type=F32)
         + jnp.dot(mi_ref[...], x_i, preferred_element_type=F32))
    o_ref[0] = y.astype(BF16).reshape(FFT_R, FFT_KB, SEC)


def _fourier_mix(f3d):
    batch, n, _ = f3d.shape
    chan, m1, m2r, m2i, twc, tws = _dft_constants()
    twc = jnp.broadcast_to(twc[:, :, None], (FFT_R, FFT_R, LANE))
    tws = jnp.broadcast_to(tws[:, :, None], (FFT_R, FFT_R, LANE))
    nb = FFT_NB
    wide = nb * SEC
    xv = f3d.reshape(batch, FFT_R, FFT_R * SEC)
    blk = pl.BlockSpec((1, FFT_R, wide), lambda b, j: (b, 0, j))
    a_spec = pl.BlockSpec((1, nb, FFT_R, SEC), lambda b, j: (b, j, 0, 0))
    tw_spec = pl.BlockSpec((nb, FFT_R, LANE), lambda b, j: (j, 0, 0))
    a_shape = jax.ShapeDtypeStruct((batch, FFT_R, FFT_R, SEC), BF16)
    a_r, a_i = pl.pallas_call(
        _fft_a_kernel,
        grid=(batch, FFT_R // nb),
        in_specs=[blk, pl.BlockSpec((LANE, 2 * LANE), lambda b, j: (0, 0)),
                  pl.BlockSpec((2 * FFT_R, 2 * FFT_R), lambda b, j: (0, 0)), tw_spec, tw_spec],
        out_specs=[a_spec, a_spec],
        out_shape=[a_shape, a_shape],
        compiler_params=_cparams("parallel", "parallel"),
        name="fourier_stage_a",
    )(xv, chan, m1, twc, tws)
    c_spec = pl.BlockSpec((1, FFT_R, FFT_KB, SEC), lambda b, j: (b, 0, j, 0))
    m_spec = pl.BlockSpec((FFT_R * FFT_KB, FFT_R * FFT_KB), lambda b, j: (0, 0))
    out = pl.pallas_call(
        _fft_c_kernel,
        grid=(batch, FFT_R // FFT_KB),
        in_specs=[c_spec, c_spec, m_spec, m_spec],
        out_specs=c_spec,
        out_shape=a_shape,
        compiler_params=_cparams("parallel", "parallel"),
        name="fourier_stage_c",
    )(a_r, a_i, m2r, m2i)
    return out.reshape(batch, n, SEC)


def _rope_tables(n):
    rows = jnp.repeat(jnp.arange(n // GRID_W, dtype=F32), GRID_W)
    cols = jnp.tile(jnp.arange(GRID_W, dtype=F32), n // GRID_W)
    inv = ROPE_BASE ** (-jnp.arange(ROPE_FREQS, dtype=F32) / ROPE_FREQS)
    lane = np.arange(HEAD_W)
    freq = lane % ROPE_FREQS
    by_col = (lane % QK_DIM) // ROPE_HALF == 1
    second = (lane % ROPE_HALF) // ROPE_FREQS == 1
    ang = jnp.where(by_col[None, :], (cols[:, None] * inv)[:, freq], (rows[:, None] * inv)[:, freq])
    cos, sin = jnp.cos(ang), jnp.sin(ang)
    zero = jnp.zeros_like(sin)
    return cos, jnp.where(second[None, :], zero, -sin), jnp.where(second[None, :], sin, zero)


def kernel(x, c, ctx, c_ctx, mod_w, mod_b, post_mix_g, post_ffn_g, ffn_w_gate, ffn_w_up, ffn_w_down, ab_w_in, ab_w_out, ab_lam_q1, ab_lam_k1, ab_lam_q2, ab_lam_k2, ab_subln_g, ab_vnorm_g, ab_vnorm_b, ab_w_spatial, ab_b_spatial, cd_w_in, cd_w_out, cd_dw_w, cd_dw_b, cd_norm_g, cd_norm_b):
    batch, n, d = x.shape
    m = ctx.shape[1]
    depth = mod_w.shape[0]
    assert (n, d, m) == (SEQ, D_MODEL, CTX_LEN) and batch <= CTX_ROW

    cond = jnp.concatenate([c, c_ctx[None, :], jnp.zeros((N_COND - batch - 1, d), F32)], axis=0)
    mod3 = _modulation(cond, mod_w, mod_b).reshape(depth * N_COND * N_MOD, 1, d)
    rope_tabs = _rope_tables(n)
    stacks = [ffn_w_gate, ffn_w_up, ffn_w_down, ab_w_out, cd_w_in, cd_w_out]
    cast_later = [w.reshape(-1, w.shape[-1]) for w in stacks]

    x_lat = x.reshape(batch * n, d)
    x_ctx = ctx.reshape(batch * m, d)
    for l in range(depth):
        last = l == depth - 1
        even = l % 2 == 0
        i = l // 2
        use_ctx = (not last) or even
        if even:
            lam_init = 0.8 - 0.6 * math.exp(-0.3 * l)
            assert l == 0, "the weight-cast side job is attached to the first layer's attention"
            w_in = ab_w_in[i].astype(BF16)
            w_sp = ab_w_spatial[i].astype(BF16)
            b_sp = ab_b_spatial[i].reshape(SEC // LANE, CHUNK, 1)
            lam_tab = jnp.zeros((N_COND, LANE), F32).at[0:4, 0:QK_DIM].set(
                jnp.stack([ab_lam_q1[i], ab_lam_k1[i], ab_lam_q2[i], ab_lam_k2[i]]))
            q, k, v, u, vn = _inproj_ab(x_lat, mod3, l, _lat_row, w_in, ab_vnorm_g[i],
                                        ab_vnorm_b[i], rope_tabs)
            qc, kc, vc, uc, vnc = _inproj_ab(x_ctx, mod3, l, _ctx_row, w_in, ab_vnorm_g[i],
                                             ab_vnorm_b[i], None)
            q3, k3, v3 = (t.reshape(batch, n, SEC) for t in (q, k, v))
            kc3, vc3 = kc.reshape(batch, m, SEC), vc.reshape(batch, m, SEC)
            a_lat, cast = _diff_attention(q3, [k3, kc3], [v3, vc3], lam_tab, ab_subln_g[i], lam_init,
                                          to_cast=cast_later)
            wg, wu, wd, ab_w_out_b, cd_w_in_b, cd_w_out_b = (
                c2.reshape(w.shape) for c2, w in zip(cast, stacks))
            w_out = ab_w_out_b[i]
            x_lat_mix = _outproj_ab(a_lat.reshape(batch * n, SEC), u, vn, w_sp, b_sp, w_out, x_lat,
                                    mod3, l, _lat_row, post_mix_g[l])
            if not last:
                a_ctx, _ = _diff_attention(qc.reshape(batch, m, SEC), [kc3], [vc3], lam_tab,
                                           ab_subln_g[i], lam_init)
                x_ctx_mix = _outproj_ab(a_ctx.reshape(batch * m, SEC), uc, vnc, w_sp, b_sp, w_out,
                                        x_ctx, mod3, l, _ctx_row, post_mix_g[l])
        else:
            w_in, w_out = cd_w_in_b[i], cd_w_out_b[i]

            def mix_cd(x2d, rows_per_seq, rowfn_maker):
                y, f = _inproj_cd(x2d, mod3, l, rowfn_maker, w_in)
                nb = x2d.shape[0] // rows_per_seq
                yc = _conformer_conv(y.reshape(nb, rows_per_seq, SEC), cd_dw_w[i], cd_dw_b[i],
                                     cd_norm_g[i], cd_norm_b[i])
                yd = _fourier_mix(f.reshape(nb, rows_per_seq, SEC))
                return _outproj_cd(yc.reshape(-1, SEC), yd.reshape(-1, SEC), w_out, x2d, mod3, l,
                                   rowfn_maker, post_mix_g[l])

            x_lat_mix = mix_cd(x_lat, n, _lat_row)
            if use_ctx:
                raise NotImplementedError("odd non-final layers are outside this problem's depth")
        x_lat = _ffn(x_lat_mix, mod3, l, _lat_row, post_ffn_g[l], wg, wu, wd)
        if not last:
            x_ctx = _ffn(x_ctx_mix, mod3, l, _ctx_row, post_ffn_g[l], wg, wu, wd)
    return x_lat.reshape(batch, n, d)
```

```python
import functools
import math

import numpy as np
import jax
import jax.numpy as jnp
from jax import lax
from jax.experimental import pallas as pl
from jax.experimental.pallas import tpu as pltpu

F32 = jnp.float32
BF16 = jnp.bfloat16

D_MODEL = 2048
SEQ = 4096
GRID_W = 64
CTX_LEN = 256
EPS = 1e-6
N_MOD = 6
D_FF = 5632
HEADS = 8
QK_DIM = 64
HEAD_W = 128
SEC = 1024
CHUNK = 128
CONV_W = 31
CONV_PAD = (CONV_W - 1) // 2
ROPE_HALF = 32
ROPE_FREQS = 16
ROPE_BASE = 10000.0
FFT_R = 64
N_COND = 8
CTX_ROW = 4

LANE = 128
SUBLANE = 8
VMEM_LIMIT = 56 * 1024 * 1024
TM = 512
TF = 512
TQ = 256
CK = 512
CONV_T = 512
CONV_RB = 64
HALO = 16
FFT_KB = 16
CAST_SPLIT = 2
FFT_NB = 8


def _cparams(*sem):
    return pltpu.CompilerParams(dimension_semantics=sem, vmem_limit_bytes=VMEM_LIMIT)


def _resident():
    return pl.BlockSpec(memory_space=pltpu.VMEM)


def _mod_spec(layer, which, rowfn):
    return pl.BlockSpec((1, 1, D_MODEL),
                        lambda i, *_: ((layer * N_COND + rowfn(i)) * N_MOD + which, 0, 0))


def _lat_row(tm):
    return lambda i: i // (SEQ // tm)


def _ctx_row(_tm):
    return lambda i: CTX_ROW


def _divmod_nonneg(x, n):
    if n & (n - 1) == 0:
        return jnp.right_shift(x, n.bit_length() - 1), jnp.bitwise_and(x, n - 1)
    return x // n, x % n


def _rms(x):
    return x * lax.rsqrt(jnp.mean(x * x, axis=-1, keepdims=True) + EPS)


def _mod_kernel(c_ref, w_ref, b_ref, o_ref):
    s = jax.nn.silu(c_ref[...]).astype(BF16)
    o_ref[0] = jnp.dot(s, w_ref[0].astype(BF16), preferred_element_type=F32) + b_ref[0]


def _modulation(cond, mod_w, mod_b):
    depth, _, n = mod_w.shape
    tn = 1024
    return pl.pallas_call(
        _mod_kernel,
        grid=(depth, n // tn),
        in_specs=[pl.BlockSpec((N_COND, D_MODEL), lambda l, j: (0, 0)),
                  pl.BlockSpec((1, D_MODEL, tn), lambda l, j: (l, 0, j)),
                  pl.BlockSpec((1, 1, tn), lambda l, j: (l, 0, j))],
        out_specs=pl.BlockSpec((1, N_COND, tn), lambda l, j: (l, 0, j)),
        out_shape=jax.ShapeDtypeStruct((depth, N_COND, n), F32),
        compiler_params=_cparams("parallel", "parallel"),
        name="adaln_mod",
    )(cond, mod_w, mod_b.reshape(depth, 1, n))


SEC_Q, SEC_K, SEC_V, SEC_U, SEC_VG = range(5)


def _inproj_ab_kernel(*refs, rope, sections):
    n_in = 9 if rope else 6
    x_ref, sh_ref, sc_ref, w_ref, vg_ref, vb_ref = refs[:6]
    if rope:
        cos_ref, s1_ref, s2_ref = refs[6:9]
    out = dict(zip(sections, refs[n_in:]))
    h = _rms(x_ref[...]) * (1.0 + sc_ref[0]) + sh_ref[0]
    hb = h.astype(BF16)

    def section(s):
        return jnp.dot(hb, w_ref[:, s * SEC:(s + 1) * SEC], preferred_element_type=F32)

    def rotate(z):
        if not rope:
            return z
        cos, s1, s2 = cos_ref[...], s1_ref[...], s2_ref[...]
        outs = []
        for hd in range(HEADS):
            zs = z[:, hd * HEAD_W:(hd + 1) * HEAD_W]
            outs.append(zs * cos + pltpu.roll(zs, HEAD_W - ROPE_FREQS, 1) * s1
                        + pltpu.roll(zs, ROPE_FREQS, 1) * s2)
        return jnp.concatenate(outs, axis=1)

    if SEC_VG in out:
        g = jax.nn.gelu(section(SEC_VG))
        outs = []
        for gi in range(SEC // LANE):
            sl = slice(gi * LANE, (gi + 1) * LANE)
            gs = g[:, sl]
            d = gs - jnp.mean(gs, axis=-1, keepdims=True)
            var = jnp.mean(d * d, axis=-1, keepdims=True)
            outs.append(d * lax.rsqrt(var + EPS) * vg_ref[:, sl] + vb_ref[:, sl])
        out[SEC_VG][...] = jnp.concatenate(outs, axis=1).astype(BF16)
    if SEC_U in out:
        out[SEC_U][...] = jax.nn.gelu(section(SEC_U)).astype(BF16)
    if SEC_Q in out:
        out[SEC_Q][...] = (rotate(section(SEC_Q)) * (QK_DIM ** -0.5)).astype(BF16)
    if SEC_K in out:
        out[SEC_K][...] = rotate(section(SEC_K)).astype(BF16)
    if SEC_V in out:
        out[SEC_V][...] = section(SEC_V).astype(BF16)


def _inproj_ab(x2d, mod3, layer, rowfn_maker, w_in, vnorm_g, vnorm_b, rope_tabs,
               sections=(SEC_Q, SEC_K, SEC_V, SEC_U, SEC_VG), tag=""):
    rows = x2d.shape[0]
    tm = TM
    rowfn = rowfn_maker(tm)
    rope = rope_tabs is not None
    row_spec = pl.BlockSpec((tm, D_MODEL), lambda i: (i, 0))
    sec_spec = pl.BlockSpec((tm, SEC), lambda i: (i, 0))
    vec_spec = pl.BlockSpec((1, SEC), lambda i: (0, 0))
    in_specs = [row_spec, _mod_spec(layer, 0, rowfn), _mod_spec(layer, 1, rowfn), _resident(),
                vec_spec, vec_spec]
    args = [x2d, mod3, mod3, w_in, vnorm_g.reshape(1, SEC), vnorm_b.reshape(1, SEC)]
    if rope:
        tab_spec = pl.BlockSpec((tm, HEAD_W), lambda i: (i % (SEQ // tm), 0))
        in_specs += [tab_spec] * 3
        args += list(rope_tabs)
    out = jax.ShapeDtypeStruct((rows, SEC), BF16)
    return pl.pallas_call(
        functools.partial(_inproj_ab_kernel, rope=rope, sections=tuple(sections)),
        grid=(rows // tm,),
        in_specs=in_specs,
        out_specs=[sec_spec] * len(sections),
        out_shape=[out] * len(sections),
        compiler_params=_cparams("parallel"),
        name=("inproj_ab_lat" if rope else "inproj_ab_ctx") + tag,
    )(*args)


def _attn_kernel(*refs, seg_lens, tq, lam_init, n_cast, cast_period, nblk, n_blocks):
    nseg = len(seg_lens)
    q_ref = refs[0]
    k_refs = refs[1:1 + nseg]
    v_refs = refs[1 + nseg:1 + 2 * nseg]
    lam_ref, g_ref = refs[1 + 2 * nseg:3 + 2 * nseg]
    cast_in = refs[3 + 2 * nseg:3 + 2 * nseg + n_cast]
    o_ref = refs[3 + 2 * nseg + n_cast]
    cast_out = refs[4 + 2 * nseg + n_cast:4 + 2 * nseg + 2 * n_cast]
    vt_scr, s_even, s_odd, m_scr = refs[4 + 2 * nseg + 2 * n_cast:]

    t = pl.program_id(0)
    if n_cast:
        @pl.when(jnp.logical_and(t % cast_period == 0, t < n_blocks))
        def _():
            for src, dst in zip(cast_in, cast_out):
                dst[...] = src[...].astype(BF16)

    chunks = []
    base = 0
    for seg, n_keys in enumerate(seg_lens):
        for c0 in range(0, n_keys, CK):
            chunks.append((seg, c0, base + c0, min(CK, n_keys - c0)))
        base += n_keys

    def fold(x, op):
        return op(x.reshape(x.shape[0] // 8, 8, x.shape[1]), axis=0)

    def step(s_write, s_read):
        q_t = q_ref[0].astype(F32).T
        sub = lax.broadcasted_iota(jnp.int32, (HEAD_W, tq), 0)
        zero = jnp.zeros_like(q_t)
        qq_t = jnp.concatenate([jnp.where(sub < QK_DIM, q_t, zero),
                                jnp.where(sub >= QK_DIM, q_t, zero)], axis=1).astype(BF16)
        if s_read is not None:
            m = m_scr[0:1, :]
            lsum = jnp.zeros((8, 2 * tq), F32)
            acc = jnp.zeros((HEAD_W, 2 * tq), F32)
        mx = None
        for seg, c0, r0, w in chunks:
            s = jnp.dot(k_refs[seg][0, c0:c0 + w, :], qq_t, preferred_element_type=F32)
            s_write[r0:r0 + w, :] = s
            t = fold(s, jnp.max)
            mx = t if mx is None else jnp.maximum(mx, t)
            if s_read is not None:
                p = jnp.exp(s_read[r0:r0 + w, :] - m)
                lsum = lsum + fold(p, jnp.sum)
                acc = acc + jnp.dot(vt_scr[:, r0:r0 + w], p.astype(BF16),
                                    preferred_element_type=F32)
        if s_read is not None:
            o_all = acc / jnp.sum(lsum, axis=0, keepdims=True)
            lam_t = lam_ref[...]
            e1 = jnp.exp(jnp.sum(lam_t[0:1, :] * lam_t[1:2, :], axis=-1, keepdims=True))
            e2 = jnp.exp(jnp.sum(lam_t[2:3, :] * lam_t[3:4, :], axis=-1, keepdims=True))
            lam = e1 - e2 + lam_init
            o_t = o_all[:, :tq] - lam * o_all[:, tq:]
            o_t = o_t * lax.rsqrt(jnp.mean(o_t * o_t, axis=0, keepdims=True) + EPS)
            o_t = o_t * g_ref[...] * (1.0 - lam_init)
            o_ref[0] = o_t.T.astype(BF16)
        m_scr[...] = jnp.broadcast_to(jnp.max(mx, axis=0, keepdims=True), m_scr.shape)

    @pl.when(jnp.logical_and(t > 0, (t + nblk - 1) % nblk == 0))
    def _():
        for seg, c0, r0, w in chunks:
            vt_scr[:, r0:r0 + w] = v_refs[seg][0, c0:c0 + w, :].astype(F32).T.astype(BF16)

    @pl.when(t == 0)
    def _():
        step(s_even, None)

    @pl.when(jnp.logical_and(t > 0, t % 2 == 1))
    def _():
        step(s_odd, s_even)

    @pl.when(jnp.logical_and(t > 0, t % 2 == 0))
    def _():
        step(s_even, s_odd)


def _diff_attention(q, ks, vs, lam_tab, subln_g, lam_init, to_cast=()):
    batch, nq, _ = q.shape
    tq = min(TQ, nq)
    nblk = nq // tq
    n_blocks = batch * HEADS * nblk
    seg_lens = tuple(int(k.shape[1]) for k in ks)
    n_keys = sum(seg_lens)

    def where(blk):
        head, qb = _divmod_nonneg(blk, nblk)
        b, h = _divmod_nonneg(head, HEADS)
        return b, qb, h

    def scored(t):
        return jnp.minimum(t, n_blocks - 1)

    def finished(t):
        return jnp.maximum(t - 1, 0)

    q_spec = pl.BlockSpec((1, tq, HEAD_W), lambda t: where(scored(t)))
    o_spec = pl.BlockSpec((1, tq, HEAD_W), lambda t: where(finished(t)))

    def kv_spec(n, which):
        def index(t):
            b, _, h = where(which(t))
            return b, 0, h
        return pl.BlockSpec((1, n, HEAD_W), index)

    s_shape = pltpu.VMEM((n_keys, 2 * tq), F32)
    cast_period = -(-nblk // CAST_SPLIT)
    n_cast_blocks = -(-n_blocks // cast_period)
    cast_specs = []
    for w in to_cast:
        rows, cols = w.shape
        assert rows % (n_cast_blocks * 2 * SUBLANE) == 0, (rows, n_cast_blocks)
        cast_specs.append(pl.BlockSpec((rows // n_cast_blocks, cols),
                                       lambda t: (_divmod_nonneg(scored(t), cast_period)[0], 0)))
    const = lambda t: (0, 0)
    outs = pl.pallas_call(
        functools.partial(_attn_kernel, seg_lens=seg_lens, tq=tq, lam_init=lam_init,
                          n_cast=len(to_cast), cast_period=cast_period, nblk=nblk,
                          n_blocks=n_blocks),
        grid=(n_blocks + 1,),
        in_specs=[q_spec] + [kv_spec(n, scored) for n in seg_lens]
                 + [kv_spec(n, finished) for n in seg_lens]
                 + [pl.BlockSpec((N_COND, LANE), const), pl.BlockSpec((HEAD_W, 1), const)]
                 + cast_specs,
        out_specs=[o_spec] + cast_specs,
        out_shape=[jax.ShapeDtypeStruct(q.shape, BF16)]
                  + [jax.ShapeDtypeStruct(w.shape, BF16) for w in to_cast],
        scratch_shapes=[pltpu.VMEM((HEAD_W, n_keys), BF16), s_shape, s_shape,
                        pltpu.VMEM((8, 2 * tq), F32)],
        compiler_params=_cparams("arbitrary"),
        name="diff_attn_%d" % nq,
    )(q, *ks, *vs, lam_tab, subln_g.reshape(HEAD_W, 1), *to_cast)
    return outs[0], tuple(outs[1:])


def _mix_epilogue(y, x_ref, gate_ref, pg_ref, o_ref):
    o_ref[...] = x_ref[...] + gate_ref[0] * (_rms(y) * pg_ref[...])


def _outproj_ab_kernel(a_ref, u_ref, vn_ref, wsp_ref, bsp_ref, wo_ref, x_ref, gate_ref, pg_ref,
                       o_ref, s_scr, *, tm):
    y = jnp.dot(a_ref[...], wo_ref[0:SEC, :], preferred_element_type=F32)
    for c in range(tm // CHUNK):
        rs = slice(c * CHUNK, (c + 1) * CHUNK)
        for g in range(SEC // LANE):
            cs = slice(g * LANE, (g + 1) * LANE)
            sv = jnp.dot(wsp_ref[g], vn_ref[rs, cs], preferred_element_type=F32) + bsp_ref[g]
            s_scr[rs, cs] = (u_ref[rs, cs].astype(F32) * sv).astype(BF16)
    y = y + jnp.dot(s_scr[...], wo_ref[SEC:2 * SEC, :], preferred_element_type=F32)
    _mix_epilogue(y, x_ref, gate_ref, pg_ref, o_ref)


def _outproj_ab(a2d, u2d, vn2d, w_sp, b_sp, w_out, x2d, mod3, layer, rowfn_maker, post_g):
    rows = x2d.shape[0]
    tm = TM
    rowfn = rowfn_maker(tm)
    sec_spec = pl.BlockSpec((tm, SEC), lambda i: (i, 0))
    row_spec = pl.BlockSpec((tm, D_MODEL), lambda i: (i, 0))
    return pl.pallas_call(
        functools.partial(_outproj_ab_kernel, tm=tm),
        grid=(rows // tm,),
        in_specs=[sec_spec, sec_spec, sec_spec, _resident(), _resident(), _resident(), row_spec,
                  _mod_spec(layer, 2, rowfn), pl.BlockSpec((1, D_MODEL), lambda i: (0, 0))],
        out_specs=row_spec,
        out_shape=jax.ShapeDtypeStruct((rows, D_MODEL), F32),
        scratch_shapes=[pltpu.VMEM((tm, SEC), BF16)],
        compiler_params=_cparams("parallel"),
        name="outproj_ab_%d" % rows,
    )(a2d, u2d, vn2d, w_sp, b_sp, w_out, x2d, mod3, post_g.reshape(1, D_MODEL))


def _outproj_cd_kernel(yc_ref, yd_ref, wo_ref, x_ref, gate_ref, pg_ref, o_ref):
    y = (jnp.dot(yc_ref[...], wo_ref[0:SEC, :], preferred_element_type=F32)
         + jnp.dot(yd_ref[...], wo_ref[SEC:2 * SEC, :], preferred_element_type=F32))
    _mix_epilogue(y, x_ref, gate_ref, pg_ref, o_ref)


def _outproj_cd(yc2d, yd2d, w_out, x2d, mod3, layer, rowfn_maker, post_g):
    rows = x2d.shape[0]
    tm = TM
    rowfn = rowfn_maker(tm)
    sec_spec = pl.BlockSpec((tm, SEC), lambda i: (i, 0))
    row_spec = pl.BlockSpec((tm, D_MODEL), lambda i: (i, 0))
    return pl.pallas_call(
        _outproj_cd_kernel,
        grid=(rows // tm,),
        in_specs=[sec_spec, sec_spec, _resident(), row_spec, _mod_spec(layer, 2, rowfn),
                  pl.BlockSpec((1, D_MODEL), lambda i: (0, 0))],
        out_specs=row_spec,
        out_shape=jax.ShapeDtypeStruct((rows, D_MODEL), F32),
        compiler_params=_cparams("parallel"),
        name="outproj_cd",
    )(yc2d, yd2d, w_out, x2d, mod3, post_g.reshape(1, D_MODEL))


def _ffn_kernel(x_ref, sh_ref, sc_ref, gate_ref, pg_ref, wg_ref, wu_ref, wd_ref, o_ref,
                hb_scr, acc_scr):
    j = pl.program_id(1)
    last = pl.num_programs(1) - 1

    def chunk():
        hb = hb_scr[...]
        g = jnp.dot(hb, wg_ref[0], preferred_element_type=F32)
        u = jnp.dot(hb, wu_ref[0], preferred_element_type=F32)
        a = (jax.nn.silu(g) * u).astype(BF16)
        return jnp.dot(a, wd_ref[0], preferred_element_type=F32)

    @pl.when(j == 0)
    def _():
        h = _rms(x_ref[...]) * (1.0 + sc_ref[0]) + sh_ref[0]
        hb_scr[...] = h.astype(BF16)
        acc_scr[...] = chunk()

    @pl.when(jnp.logical_and(j > 0, j < last))
    def _():
        acc_scr[...] += chunk()

    @pl.when(j == last)
    def _():
        _mix_epilogue(acc_scr[...] + chunk(), x_ref, gate_ref, pg_ref, o_ref)


def _ffn(x2d, mod3, layer, rowfn_maker, post_g, w_gate, w_up, w_down):
    rows = x2d.shape[0]
    tm = TM
    rowfn = rowfn_maker(tm)
    row_spec = pl.BlockSpec((tm, D_MODEL), lambda i, j: (i, 0))
    return pl.pallas_call(
        _ffn_kernel,
        grid=(rows // tm, D_FF // TF),
        in_specs=[row_spec, _mod_spec(layer, 3, rowfn), _mod_spec(layer, 4, rowfn),
                  _mod_spec(layer, 5, rowfn), pl.BlockSpec((1, D_MODEL), lambda i, j: (0, 0)),
                  pl.BlockSpec((1, D_MODEL, TF), lambda i, j: (layer, 0, j)),
                  pl.BlockSpec((1, D_MODEL, TF), lambda i, j: (layer, 0, j)),
                  pl.BlockSpec((1, TF, D_MODEL), lambda i, j: (layer, j, 0))],
        out_specs=row_spec,
        out_shape=jax.ShapeDtypeStruct((rows, D_MODEL), F32),
        scratch_shapes=[pltpu.VMEM((tm, D_MODEL), BF16), pltpu.VMEM((tm, D_MODEL), F32)],
        compiler_params=_cparams("parallel", "arbitrary"),
        name="ffn_%d" % rows,
    )(x2d, mod3, mod3, mod3, post_g.reshape(1, D_MODEL), w_gate, w_up, w_down)


def _inproj_cd_kernel(x_ref, sh_ref, sc_ref, w_ref, y_ref, f_ref):
    h = _rms(x_ref[...]) * (1.0 + sc_ref[0]) + sh_ref[0]
    hb = h.astype(BF16)

    def section(s):
        return jnp.dot(hb, w_ref[:, s * SEC:(s + 1) * SEC], preferred_element_type=F32)

    y_ref[...] = section(0) * jax.nn.sigmoid(section(1))
    f_ref[...] = section(2).astype(BF16)


def _inproj_cd(x2d, mod3, layer, rowfn_maker, w_in):
    rows = x2d.shape[0]
    tm = TM
    rowfn = rowfn_maker(tm)
    sec_spec = pl.BlockSpec((tm, SEC), lambda i: (i, 0))
    return pl.pallas_call(
        _inproj_cd_kernel,
        grid=(rows // tm,),
        in_specs=[pl.BlockSpec((tm, D_MODEL), lambda i: (i, 0)), _mod_spec(layer, 0, rowfn),
                  _mod_spec(layer, 1, rowfn), _resident()],
        out_specs=[sec_spec, sec_spec],
        out_shape=[jax.ShapeDtypeStruct((rows, SEC), F32), jax.ShapeDtypeStruct((rows, SEC), BF16)],
        compiler_params=_cparams("parallel"),
        name="inproj_cd",
    )(x2d, mod3, mod3, w_in)


def _conv_kernel(y_ref, prev_ref, next_ref, w_ref, b_ref, g_ref, be_ref, o_ref, ybuf, cbuf):
    i = pl.program_id(1)
    t = y_ref.shape[1]
    zeros = jnp.zeros((HALO, SEC), F32)
    ybuf[0:HALO, :] = jnp.where(i > 0, prev_ref[0], zeros)
    ybuf[HALO:HALO + t, :] = y_ref[0]
    ybuf[HALO + t:HALO + t + HALO, :] = jnp.where(i < pl.num_programs(1) - 1, next_ref[0], zeros)
    off = HALO - CONV_PAD

    def conv_block(rb, carry):
        r0 = pl.multiple_of(rb * CONV_RB, CONV_RB)
        for c in range(SEC // LANE):
            cs = slice(c * LANE, (c + 1) * LANE)
            win = ybuf[pl.ds(r0, CONV_RB + 2 * HALO), cs]
            acc = jnp.zeros((CONV_RB, LANE), F32) + b_ref[:, cs]
            n_win = CONV_RB + 2 * HALO
            for r in range(SUBLANE):
                shifted = win if r == 0 else pltpu.roll(win, n_win - r, 0)
                for a in range((2 * HALO) // SUBLANE):
                    w = a * SUBLANE + r - off
                    if 0 <= w < CONV_W:
                        acc = acc + (shifted[a * SUBLANE:a * SUBLANE + CONV_RB, :]
                                     * w_ref[w:w + 1, cs])
            cbuf[pl.ds(r0, CONV_RB), cs] = acc
        return carry

    lax.fori_loop(0, t // CONV_RB, conv_block, 0)

    def norm_block(rb, carry):
        r0 = pl.multiple_of(rb * CONV_RB, CONV_RB)
        v = cbuf[pl.ds(r0, CONV_RB), :]
        d = v - jnp.mean(v, axis=-1, keepdims=True)
        var = jnp.mean(d * d, axis=-1, keepdims=True)
        z = d * lax.rsqrt(var + EPS) * g_ref[...] + be_ref[...]
        o_ref[0, pl.ds(r0, CONV_RB), :] = jax.nn.silu(z).astype(BF16)
        return carry

    lax.fori_loop(0, t // CONV_RB, norm_block, 0)


def _conformer_conv(y3d, dw_w, dw_b, norm_g, norm_b):
    batch, n, _ = y3d.shape
    t = CONV_T
    hb = t // HALO
    nh = n // HALO
    vec = pl.BlockSpec((1, SEC), lambda b, i: (0, 0))
    return pl.pallas_call(
        _conv_kernel,
        grid=(batch, n // t),
        in_specs=[pl.BlockSpec((1, t, SEC), lambda b, i: (b, i, 0)),
                  pl.BlockSpec((1, HALO, SEC), lambda b, i: (b, jnp.maximum(i * hb - 1, 0), 0)),
                  pl.BlockSpec((1, HALO, SEC),
                               lambda b, i: (b, jnp.minimum((i + 1) * hb, nh - 1), 0)),
                  pl.BlockSpec((CONV_W, SEC), lambda b, i: (0, 0)), vec, vec, vec],
        out_specs=pl.BlockSpec((1, t, SEC), lambda b, i: (b, i, 0)),
        out_shape=jax.ShapeDtypeStruct((batch, n, SEC), BF16),
        scratch_shapes=[pltpu.VMEM((t + 2 * HALO, SEC), F32), pltpu.VMEM((t, SEC), F32)],
        compiler_params=_cparams("parallel", "arbitrary"),
        name="conformer_conv",
    )(y3d, y3d, y3d, dw_w, dw_b.reshape(1, SEC), norm_g.reshape(1, SEC), norm_b.reshape(1, SEC))


def _dft_constants():
    c = np.arange(LANE)
    ang_c = 2.0 * np.pi * np.outer(c, c) / LANE
    norm = 1.0 / math.sqrt(SEQ * LANE)
    chan = np.concatenate([np.cos(ang_c), -np.sin(ang_c)], axis=1) * norm
    r = np.arange(FFT_R)
    ang_r = 2.0 * np.pi * np.outer(r, r) / FFT_R
    cr, ci = np.cos(ang_r), -np.sin(ang_r)
    stage1 = np.block([[cr, -ci], [ci, cr]])
    eye = np.eye(FFT_KB)
    stage2_r, stage2_i = np.kron(cr, eye), np.kron(-ci, eye)
    ang_t = 2.0 * np.pi * np.outer(r, r) / SEQ
    return (jnp.asarray(chan, F32).astype(BF16), jnp.asarray(stage1, F32).astype(BF16),
            jnp.asarray(stage2_r, F32).astype(BF16), jnp.asarray(stage2_i, F32).astype(BF16),
            jnp.asarray(np.cos(ang_t), F32), jnp.asarray(-np.sin(ang_t), F32))


def _fft_a_kernel(x_ref, chan_ref, m1_ref, twc_ref, tws_ref, ar_ref, ai_ref):
    x = x_ref[0]
    nslab = x.shape[1] // LANE
    xs = jnp.concatenate([x[:, s * LANE:(s + 1) * LANE] for s in range(nslab)], axis=0)
    z = jnp.dot(xs, chan_ref[...], preferred_element_type=F32).astype(BF16)
    zr = jnp.concatenate([z[s * FFT_R:(s + 1) * FFT_R, 0:LANE] for s in range(nslab)], axis=1)
    zi = jnp.concatenate([z[s * FFT_R:(s + 1) * FFT_R, LANE:2 * LANE] for s in range(nslab)],
                         axis=1)
    a = jnp.dot(m1_ref[...], jnp.concatenate([zr, zi], axis=0), preferred_element_type=F32)
    a_r, a_i = a[0:FFT_R], a[FFT_R:2 * FFT_R]
    for j in range(x.shape[1] // SEC):
        cs = slice(j * SEC, (j + 1) * SEC)
        tc = jnp.concatenate([twc_ref[j]] * (SEC // LANE), axis=1)
        ts = jnp.concatenate([tws_ref[j]] * (SEC // LANE), axis=1)
        ar_ref[0, j] = (a_r[:, cs] * tc - a_i[:, cs] * ts).astype(BF16)
        ai_ref[0, j] = (a_r[:, cs] * ts + a_i[:, cs] * tc).astype(BF16)


def _fft_c_kernel(ar_ref, ai_ref, mr_ref, mi_ref, o_ref):
    rows = FFT_R * FFT_KB
    x_r = ar_ref[0].reshape(rows, SEC)
    x_i = ai_ref[0].reshape(rows, SEC)
    y = (jnp.dot(mr_ref[...], x_r, preferred_element_type=F32)
         + jnp.dot(mi_ref[...], x_i, preferred_element_type=F32))
    o_ref[0] = y.astype(BF16).reshape(FFT_R, FFT_KB, SEC)


def _fourier_mix(f3d):
    batch, n, _ = f3d.shape
    chan, m1, m2r, m2i, twc, tws = _dft_constants()
    twc = jnp.broadcast_to(twc[:, :, None], (FFT_R, FFT_R, LANE))
    tws = jnp.broadcast_to(tws[:, :, None], (FFT_R, FFT_R, LANE))
    nb = FFT_NB
    wide = nb * SEC
    xv = f3d.reshape(batch, FFT_R, FFT_R * SEC)
    blk = pl.BlockSpec((1, FFT_R, wide), lambda b, j: (b, 0, j))
    a_spec = pl.BlockSpec((1, nb, FFT_R, SEC), lambda b, j: (b, j, 0, 0))
    tw_spec = pl.BlockSpec((nb, FFT_R, LANE), lambda b, j: (j, 0, 0))
    a_shape = jax.ShapeDtypeStruct((batch, FFT_R, FFT_R, SEC), BF16)
    a_r, a_i = pl.pallas_call(
        _fft_a_kernel,
        grid=(batch, FFT_R // nb),
        in_specs=[blk, pl.BlockSpec((LANE, 2 * LANE), lambda b, j: (0, 0)),
                  pl.BlockSpec((2 * FFT_R, 2 * FFT_R), lambda b, j: (0, 0)), tw_spec, tw_spec],
        out_specs=[a_spec, a_spec],
        out_shape=[a_shape, a_shape],
        compiler_params=_cparams("parallel", "parallel"),
        name="fourier_stage_a",
    )(xv, chan, m1, twc, tws)
    c_spec = pl.BlockSpec((1, FFT_R, FFT_KB, SEC), lambda b, j: (b, 0, j, 0))
    m_spec = pl.BlockSpec((FFT_R * FFT_KB, FFT_R * FFT_KB), lambda b, j: (0, 0))
    out = pl.pallas_call(
        _fft_c_kernel,
        grid=(batch, FFT_R // FFT_KB),
        in_specs=[c_spec, c_spec, m_spec, m_spec],
        out_specs=c_spec,
        out_shape=a_shape,
        compiler_params=_cparams("parallel", "parallel"),
        name="fourier_stage_c",
    )(a_r, a_i, m2r, m2i)
    return out.reshape(batch, n, SEC)


def _rope_tables(n):
    rows = jnp.repeat(jnp.arange(n // GRID_W, dtype=F32), GRID_W)
    cols = jnp.tile(jnp.arange(GRID_W, dtype=F32), n // GRID_W)
    inv = ROPE_BASE ** (-jnp.arange(ROPE_FREQS, dtype=F32) / ROPE_FREQS)
    lane = np.arange(HEAD_W)
    freq = lane % ROPE_FREQS
    by_col = (lane % QK_DIM) // ROPE_HALF == 1
    second = (lane % ROPE_HALF) // ROPE_FREQS == 1
    ang = jnp.where(by_col[None, :], (cols[:, None] * inv)[:, freq], (rows[:, None] * inv)[:, freq])
    cos, sin = jnp.cos(ang), jnp.sin(ang)
    zero = jnp.zeros_like(sin)
    return cos, jnp.where(second[None, :], zero, -sin), jnp.where(second[None, :], sin, zero)


def kernel(x, c, ctx, c_ctx, mod_w, mod_b, post_mix_g, post_ffn_g, ffn_w_gate, ffn_w_up, ffn_w_down, ab_w_in, ab_w_out, ab_lam_q1, ab_lam_k1, ab_lam_q2, ab_lam_k2, ab_subln_g, ab_vnorm_g, ab_vnorm_b, ab_w_spatial, ab_b_spatial, cd_w_in, cd_w_out, cd_dw_w, cd_dw_b, cd_norm_g, cd_norm_b):
    batch, n, d = x.shape
    m = ctx.shape[1]
    depth = mod_w.shape[0]
    assert (n, d, m) == (SEQ, D_MODEL, CTX_LEN) and batch <= CTX_ROW

    cond = jnp.concatenate([c, c_ctx[None, :], jnp.zeros((N_COND - batch - 1, d), F32)], axis=0)
    mod3 = _modulation(cond, mod_w, mod_b).reshape(depth * N_COND * N_MOD, 1, d)
    rope_tabs = _rope_tables(n)
    stacks = [ffn_w_gate, ffn_w_up, ffn_w_down, ab_w_out, cd_w_in, cd_w_out]
    cast_later = [w.reshape(-1, w.shape[-1]) for w in stacks]

    x_lat = x.reshape(batch * n, d)
    x_ctx = ctx.reshape(batch * m, d)
    for l in range(depth):
        last = l == depth - 1
        even = l % 2 == 0
        i = l // 2
        use_ctx = (not last) or even
        if even:
            lam_init = 0.8 - 0.6 * math.exp(-0.3 * l)
            assert l == 0, "the weight-cast side job is attached to the first layer's attention"
            w_in = ab_w_in[i].astype(BF16)
            w_sp = ab_w_spatial[i].astype(BF16)
            b_sp = ab_b_spatial[i].reshape(SEC // LANE, CHUNK, 1)
            lam_tab = jnp.zeros((N_COND, LANE), F32).at[0:4, 0:QK_DIM].set(
                jnp.stack([ab_lam_q1[i], ab_lam_k1[i], ab_lam_q2[i], ab_lam_k2[i]]))
            q, k, v, u, vn = _inproj_ab(x_lat, mod3, l, _lat_row, w_in, ab_vnorm_g[i],
                                        ab_vnorm_b[i], rope_tabs)
            kc, vc = _inproj_ab(x_ctx, mod3, l, _ctx_row, w_in, ab_vnorm_g[i], ab_vnorm_b[i], None,
                                sections=(SEC_K, SEC_V), tag="_kv")
            qc, uc, vnc = _inproj_ab(x_ctx, mod3, l, _ctx_row, w_in, ab_vnorm_g[i], ab_vnorm_b[i],
                                     None, sections=(SEC_Q, SEC_U, SEC_VG), tag="_rest")
            q3, k3, v3 = (t.reshape(batch, n, SEC) for t in (q, k, v))
            kc3, vc3 = kc.reshape(batch, m, SEC), vc.reshape(batch, m, SEC)
            a_lat, cast = _diff_attention(q3, [k3, kc3], [v3, vc3], lam_tab, ab_subln_g[i], lam_init,
                                          to_cast=cast_later)
            wg, wu, wd, ab_w_out_b, cd_w_in_b, cd_w_out_b = (
                c2.reshape(w.shape) for c2, w in zip(cast, stacks))
            w_out = ab_w_out_b[i]
            x_lat_mix = _outproj_ab(a_lat.reshape(batch * n, SEC), u, vn, w_sp, b_sp, w_out, x_lat,
                                    mod3, l, _lat_row, post_mix_g[l])
            if not last:
                a_ctx, _ = _diff_attention(qc.reshape(batch, m, SEC), [kc3], [vc3], lam_tab,
                                           ab_subln_g[i], lam_init)
                x_ctx_mix = _outproj_ab(a_ctx.reshape(batch * m, SEC), uc, vnc, w_sp, b_sp, w_out,
                                        x_ctx, mod3, l, _ctx_row, post_mix_g[l])
        else:
            w_in, w_out = cd_w_in_b[i], cd_w_out_b[i]

            def mix_cd(x2d, rows_per_seq, rowfn_maker):
                y, f = _inproj_cd(x2d, mod3, l, rowfn_maker, w_in)
                nb = x2d.shape[0] // rows_per_seq
                yc = _conformer_conv(y.reshape(nb, rows_per_seq, SEC), cd_dw_w[i], cd_dw_b[i],
                                     cd_norm_g[i], cd_norm_b[i])
                yd = _fourier_mix(f.reshape(nb, rows_per_seq, SEC))
                return _outproj_cd(yc.reshape(-1, SEC), yd.reshape(-1, SEC), w_out, x2d, mod3, l,
                                   rowfn_maker, post_mix_g[l])

            x_lat_mix = mix_cd(x_lat, n, _lat_row)
            if use_ctx:
                raise NotImplementedError("odd non-final layers are outside this problem's depth")
        x_lat = _ffn(x_lat_mix, mod3, l, _lat_row, post_ffn_g[l], wg, wu, wd)
        if not last:
            x_ctx = _ffn(x_ctx_mix, mod3, l, _ctx_row, post_ffn_g[l], wg, wu, wd)
    return x_lat.reshape(batch, n, d)
```

```python
import functools
import math

import numpy as np
import jax
import jax.numpy as jnp
from jax import lax
from jax.experimental import pallas as pl
from jax.experimental.pallas import tpu as pltpu

F32 = jnp.float32
BF16 = jnp.bfloat16

D_MODEL = 2048
SEQ = 4096
GRID_W = 64
CTX_LEN = 256
EPS = 1e-6
N_MOD = 6
D_FF = 5632
HEADS = 8
QK_DIM = 64
HEAD_W = 128
SEC = 1024
CHUNK = 128
CONV_W = 31
CONV_PAD = (CONV_W - 1) // 2
ROPE_HALF = 32
ROPE_FREQS = 16
ROPE_BASE = 10000.0
FFT_R = 64
N_COND = 8
CTX_ROW = 4

LANE = 128
SUBLANE = 8
VMEM_LIMIT = 56 * 1024 * 1024
TM = 512
TF = 512
TQ = 256
CK = 512
CONV_T = 512
CONV_RB = 64
HALO = 16
FFT_KB = 16
CAST_SPLIT = 2
FFT_NB = 8


def _cparams(*sem):
    return pltpu.CompilerParams(dimension_semantics=sem, vmem_limit_bytes=VMEM_LIMIT)


def _resident():
    return pl.BlockSpec(memory_space=pltpu.VMEM)


def _mod_spec(layer, which, rowfn):
    return pl.BlockSpec((1, 1, D_MODEL),
                        lambda i, *_: ((layer * N_COND + rowfn(i)) * N_MOD + which, 0, 0))


def _lat_row(tm):
    return lambda i: i // (SEQ // tm)


def _ctx_row(_tm):
    return lambda i: CTX_ROW


def _divmod_nonneg(x, n):
    if n & (n - 1) == 0:
        return jnp.right_shift(x, n.bit_length() - 1), jnp.bitwise_and(x, n - 1)
    return x // n, x % n


def _rms(x):
    return x * lax.rsqrt(jnp.mean(x * x, axis=-1, keepdims=True) + EPS)


def _mod_kernel(c_ref, w_ref, b_ref, o_ref):
    s = jax.nn.silu(c_ref[...]).astype(BF16)
    o_ref[0] = jnp.dot(s, w_ref[0].astype(BF16), preferred_element_type=F32) + b_ref[0]


def _modulation(cond, mod_w, mod_b):
    depth, _, n = mod_w.shape
    tn = 1024
    return pl.pallas_call(
        _mod_kernel,
        grid=(depth, n // tn),
        in_specs=[pl.BlockSpec((N_COND, D_MODEL), lambda l, j: (0, 0)),
                  pl.BlockSpec((1, D_MODEL, tn), lambda l, j: (l, 0, j)),
                  pl.BlockSpec((1, 1, tn), lambda l, j: (l, 0, j))],
        out_specs=pl.BlockSpec((1, N_COND, tn), lambda l, j: (l, 0, j)),
        out_shape=jax.ShapeDtypeStruct((depth, N_COND, n), F32),
        compiler_params=_cparams("parallel", "parallel"),
        name="adaln_mod",
    )(cond, mod_w, mod_b.reshape(depth, 1, n))


SEC_Q, SEC_K, SEC_V, SEC_U, SEC_VG = range(5)


def _inproj_ab_kernel(*refs, rope, sections):
    n_in = 9 if rope else 6
    x_ref, sh_ref, sc_ref, w_ref, vg_ref, vb_ref = refs[:6]
    if rope:
        cos_ref, s1_ref, s2_ref = refs[6:9]
    out = dict(zip(sections, refs[n_in:]))
    h = _rms(x_ref[...]) * (1.0 + sc_ref[0]) + sh_ref[0]
    hb = h.astype(BF16)

    def section(s):
        return jnp.dot(hb, w_ref[:, s * SEC:(s + 1) * SEC], preferred_element_type=F32)

    def rotate(z):
        if not rope:
            return z
        cos, s1, s2 = cos_ref[...], s1_ref[...], s2_ref[...]
        outs = []
        for hd in range(HEADS):
            zs = z[:, hd * HEAD_W:(hd + 1) * HEAD_W]
            outs.append(zs * cos + pltpu.roll(zs, HEAD_W - ROPE_FREQS, 1) * s1
                        + pltpu.roll(zs, ROPE_FREQS, 1) * s2)
        return jnp.concatenate(outs, axis=1)

    if SEC_VG in out:
        g = jax.nn.gelu(section(SEC_VG))
        outs = []
        for gi in range(SEC // LANE):
            sl = slice(gi * LANE, (gi + 1) * LANE)
            gs = g[:, sl]
            d = gs - jnp.mean(gs, axis=-1, keepdims=True)
            var = jnp.mean(d * d, axis=-1, keepdims=True)
            outs.append(d * lax.rsqrt(var + EPS) * vg_ref[:, sl] + vb_ref[:, sl])
        out[SEC_VG][...] = jnp.concatenate(outs, axis=1).astype(BF16)
    if SEC_U in out:
        out[SEC_U][...] = jax.nn.gelu(section(SEC_U)).astype(BF16)
    if SEC_Q in out:
        out[SEC_Q][...] = (rotate(section(SEC_Q)) * (QK_DIM ** -0.5)).astype(BF16)
    if SEC_K in out:
        out[SEC_K][...] = rotate(section(SEC_K)).astype(BF16)
    if SEC_V in out:
        out[SEC_V][...] = section(SEC_V).astype(BF16)


def _inproj_ab(x2d, mod3, layer, rowfn_maker, w_in, vnorm_g, vnorm_b, rope_tabs,
               sections=(SEC_Q, SEC_K, SEC_V, SEC_U, SEC_VG), tag=""):
    rows = x2d.shape[0]
    tm = TM
    rowfn = rowfn_maker(tm)
    rope = rope_tabs is not None
    row_spec = pl.BlockSpec((tm, D_MODEL), lambda i: (i, 0))
    sec_spec = pl.BlockSpec((tm, SEC), lambda i: (i, 0))
    vec_spec = pl.BlockSpec((1, SEC), lambda i: (0, 0))
    in_specs = [row_spec, _mod_spec(layer, 0, rowfn), _mod_spec(layer, 1, rowfn), _resident(),
                vec_spec, vec_spec]
    args = [x2d, mod3, mod3, w_in, vnorm_g.reshape(1, SEC), vnorm_b.reshape(1, SEC)]
    if rope:
        tab_spec = pl.BlockSpec((tm, HEAD_W), lambda i: (i % (SEQ // tm), 0))
        in_specs += [tab_spec] * 3
        args += list(rope_tabs)
    out = jax.ShapeDtypeStruct((rows, SEC), BF16)
    return pl.pallas_call(
        functools.partial(_inproj_ab_kernel, rope=rope, sections=tuple(sections)),
        grid=(rows // tm,),
        in_specs=in_specs,
        out_specs=[sec_spec] * len(sections),
        out_shape=[out] * len(sections),
        compiler_params=_cparams("parallel"),
        name=("inproj_ab_lat" if rope else "inproj_ab_ctx") + tag,
    )(*args)


def _attn_kernel(*refs, seg_lens, tq, lam_init, n_cast, cast_period, nblk, n_blocks):
    nseg = len(seg_lens)
    q_ref = refs[0]
    k_refs = refs[1:1 + nseg]
    v_refs = refs[1 + nseg:1 + 2 * nseg]
    lam_ref, g_ref = refs[1 + 2 * nseg:3 + 2 * nseg]
    cast_in = refs[3 + 2 * nseg:3 + 2 * nseg + n_cast]
    o_ref = refs[3 + 2 * nseg + n_cast]
    cast_out = refs[4 + 2 * nseg + n_cast:4 + 2 * nseg + 2 * n_cast]
    vt_scr, s_even, s_odd, m_scr = refs[4 + 2 * nseg + 2 * n_cast:]

    t = pl.program_id(0)
    if n_cast:
        @pl.when(jnp.logical_and(t % cast_period == 0, t < n_blocks))
        def _():
            for src, dst in zip(cast_in, cast_out):
                dst[...] = src[...].astype(BF16)

    chunks = []
    base = 0
    for seg, n_keys in enumerate(seg_lens):
        for c0 in range(0, n_keys, CK):
            chunks.append((seg, c0, base + c0, min(CK, n_keys - c0)))
        base += n_keys

    def fold(x, op):
        return op(x.reshape(x.shape[0] // 8, 8, x.shape[1]), axis=0)

    def step(s_write, s_read):
        q_t = q_ref[0].astype(F32).T
        sub = lax.broadcasted_iota(jnp.int32, (HEAD_W, tq), 0)
        zero = jnp.zeros_like(q_t)
        qq_t = jnp.concatenate([jnp.where(sub < QK_DIM, q_t, zero),
                                jnp.where(sub >= QK_DIM, q_t, zero)], axis=1).astype(BF16)
        if s_read is not None:
            m = m_scr[0:1, :]
            lsum = jnp.zeros((8, 2 * tq), F32)
            acc = jnp.zeros((HEAD_W, 2 * tq), F32)
        mx = None
        for seg, c0, r0, w in chunks:
            s = jnp.dot(k_refs[seg][0, c0:c0 + w, :], qq_t, preferred_element_type=F32)
            s_write[r0:r0 + w, :] = s
            t = fold(s, jnp.max)
            mx = t if mx is None else jnp.maximum(mx, t)
            if s_read is not None:
                p = jnp.exp(s_read[r0:r0 + w, :] - m)
                lsum = lsum + fold(p, jnp.sum)
                acc = acc + jnp.dot(vt_scr[:, r0:r0 + w], p.astype(BF16),
                                    preferred_element_type=F32)
        if s_read is not None:
            o_all = acc / jnp.sum(lsum, axis=0, keepdims=True)
            lam_t = lam_ref[...]
            e1 = jnp.exp(jnp.sum(lam_t[0:1, :] * lam_t[1:2, :], axis=-1, keepdims=True))
            e2 = jnp.exp(jnp.sum(lam_t[2:3, :] * lam_t[3:4, :], axis=-1, keepdims=True))
            lam = e1 - e2 + lam_init
            o_t = o_all[:, :tq] - lam * o_all[:, tq:]
            o_t = o_t * lax.rsqrt(jnp.mean(o_t * o_t, axis=0, keepdims=True) + EPS)
            o_t = o_t * g_ref[...] * (1.0 - lam_init)
            o_ref[0] = o_t.T.astype(BF16)
        m_scr[...] = jnp.broadcast_to(jnp.max(mx, axis=0, keepdims=True), m_scr.shape)

    @pl.when(jnp.logical_and(t > 0, (t + nblk - 1) % nblk == 0))
    def _():
        for seg, c0, r0, w in chunks:
            vt_scr[:, r0:r0 + w] = v_refs[seg][0, c0:c0 + w, :].astype(F32).T.astype(BF16)

    @pl.when(t == 0)
    def _():
        step(s_even, None)

    @pl.when(jnp.logical_and(t > 0, t % 2 == 1))
    def _():
        step(s_odd, s_even)

    @pl.when(jnp.logical_and(t > 0, t % 2 == 0))
    def _():
        step(s_even, s_odd)


def _diff_attention(q, ks, vs, lam_tab, subln_g, lam_init, to_cast=()):
    batch, nq, _ = q.shape
    tq = min(TQ, nq)
    nblk = nq // tq
    n_blocks = batch * HEADS * nblk
    seg_lens = tuple(int(k.shape[1]) for k in ks)
    n_keys = sum(seg_lens)

    def where(blk):
        head, qb = _divmod_nonneg(blk, nblk)
        b, h = _divmod_nonneg(head, HEADS)
        return b, qb, h

    def scored(t):
        return jnp.minimum(t, n_blocks - 1)

    def finished(t):
        return jnp.maximum(t - 1, 0)

    q_spec = pl.BlockSpec((1, tq, HEAD_W), lambda t: where(scored(t)))
    o_spec = pl.BlockSpec((1, tq, HEAD_W), lambda t: where(finished(t)))

    def kv_spec(n, which):
        def index(t):
            b, _, h = where(which(t))
            return b, 0, h
        return pl.BlockSpec((1, n, HEAD_W), index)

    s_shape = pltpu.VMEM((n_keys, 2 * tq), F32)
    cast_period = -(-nblk // CAST_SPLIT)
    n_cast_blocks = -(-n_blocks // cast_period)
    cast_specs = []
    for w in to_cast:
        rows, cols = w.shape
        assert rows % (n_cast_blocks * 2 * SUBLANE) == 0, (rows, n_cast_blocks)
        cast_specs.append(pl.BlockSpec((rows // n_cast_blocks, cols),
                                       lambda t: (_divmod_nonneg(scored(t), cast_period)[0], 0)))
    const = lambda t: (0, 0)
    outs = pl.pallas_call(
        functools.partial(_attn_kernel, seg_lens=seg_lens, tq=tq, lam_init=lam_init,
                          n_cast=len(to_cast), cast_period=cast_period, nblk=nblk,
                          n_blocks=n_blocks),
        grid=(n_blocks + 1,),
        in_specs=[q_spec] + [kv_spec(n, scored) for n in seg_lens]
                 + [kv_spec(n, finished) for n in seg_lens]
                 + [pl.BlockSpec((N_COND, LANE), const), pl.BlockSpec((HEAD_W, 1), const)]
                 + cast_specs,
        out_specs=[o_spec] + cast_specs,
        out_shape=[jax.ShapeDtypeStruct(q.shape, BF16)]
                  + [jax.ShapeDtypeStruct(w.shape, BF16) for w in to_cast],
        scratch_shapes=[pltpu.VMEM((HEAD_W, n_keys), BF16), s_shape, s_shape,
                        pltpu.VMEM((8, 2 * tq), F32)],
        compiler_params=_cparams("arbitrary"),
        name="diff_attn_%d" % nq,
    )(q, *ks, *vs, lam_tab, subln_g.reshape(HEAD_W, 1), *to_cast)
    return outs[0], tuple(outs[1:])


def _mix_epilogue(y, x_ref, gate_ref, pg_ref, o_ref):
    o_ref[...] = x_ref[...] + gate_ref[0] * (_rms(y) * pg_ref[...])


def _outproj_ab_kernel(a_ref, u_ref, vn_ref, wsp_ref, bsp_ref, wo_ref, x_ref, gate_ref, pg_ref,
                       o_ref, s_scr, *, tm):
    y = jnp.dot(a_ref[...], wo_ref[0:SEC, :], preferred_element_type=F32)
    for c in range(tm // CHUNK):
        rs = slice(c * CHUNK, (c + 1) * CHUNK)
        for g in range(SEC // LANE):
            cs = slice(g * LANE, (g + 1) * LANE)
            sv = jnp.dot(wsp_ref[g], vn_ref[rs, cs], preferred_element_type=F32) + bsp_ref[g]
            s_scr[rs, cs] = (u_ref[rs, cs].astype(F32) * sv).astype(BF16)
    y = y + jnp.dot(s_scr[...], wo_ref[SEC:2 * SEC, :], preferred_element_type=F32)
    _mix_epilogue(y, x_ref, gate_ref, pg_ref, o_ref)


def _outproj_ab(a2d, u2d, vn2d, w_sp, b_sp, w_out, x2d, mod3, layer, rowfn_maker, post_g):
    rows = x2d.shape[0]
    tm = TM
    rowfn = rowfn_maker(tm)
    sec_spec = pl.BlockSpec((tm, SEC), lambda i: (i, 0))
    row_spec = pl.BlockSpec((tm, D_MODEL), lambda i: (i, 0))
    return pl.pallas_call(
        functools.partial(_outproj_ab_kernel, tm=tm),
        grid=(rows // tm,),
        in_specs=[sec_spec, sec_spec, sec_spec, _resident(), _resident(), _resident(), row_spec,
                  _mod_spec(layer, 2, rowfn), pl.BlockSpec((1, D_MODEL), lambda i: (0, 0))],
        out_specs=row_spec,
        out_shape=jax.ShapeDtypeStruct((rows, D_MODEL), F32),
        scratch_shapes=[pltpu.VMEM((tm, SEC), BF16)],
        compiler_params=_cparams("parallel"),
        name="outproj_ab_%d" % rows,
    )(a2d, u2d, vn2d, w_sp, b_sp, w_out, x2d, mod3, post_g.reshape(1, D_MODEL))


def _outproj_cd_kernel(yc_ref, yd_ref, wo_ref, x_ref, gate_ref, pg_ref, o_ref):
    y = (jnp.dot(yc_ref[...], wo_ref[0:SEC, :], preferred_element_type=F32)
         + jnp.dot(yd_ref[...], wo_ref[SEC:2 * SEC, :], preferred_element_type=F32))
    _mix_epilogue(y, x_ref, gate_ref, pg_ref, o_ref)


def _outproj_cd(yc2d, yd2d, w_out, x2d, mod3, layer, rowfn_maker, post_g):
    rows = x2d.shape[0]
    tm = TM
    rowfn = rowfn_maker(tm)
    sec_spec = pl.BlockSpec((tm, SEC), lambda i: (i, 0))
    row_spec = pl.BlockSpec((tm, D_MODEL), lambda i: (i, 0))
    return pl.pallas_call(
        _outproj_cd_kernel,
        grid=(rows // tm,),
        in_specs=[sec_spec, sec_spec, _resident(), row_spec, _mod_spec(layer, 2, rowfn),
                  pl.BlockSpec((1, D_MODEL), lambda i: (0, 0))],
        out_specs=row_spec,
        out_shape=jax.ShapeDtypeStruct((rows, D_MODEL), F32),
        compiler_params=_cparams("parallel"),
        name="outproj_cd",
    )(yc2d, yd2d, w_out, x2d, mod3, post_g.reshape(1, D_MODEL))


def _ffn_kernel(x_ref, sh_ref, sc_ref, gate_ref, pg_ref, wg_ref, wu_ref, wd_ref, o_ref,
                hb_scr, acc_scr):
    j = pl.program_id(1)
    last = pl.num_programs(1) - 1

    def chunk():
        hb = hb_scr[...]
        g = jnp.dot(hb, wg_ref[0], preferred_element_type=F32)
        u = jnp.dot(hb, wu_ref[0], preferred_element_type=F32)
        a = (jax.nn.silu(g) * u).astype(BF16)
        return jnp.dot(a, wd_ref[0], preferred_element_type=F32)

    @pl.when(j == 0)
    def _():
        h = _rms(x_ref[...]) * (1.0 + sc_ref[0]) + sh_ref[0]
        hb_scr[...] = h.astype(BF16)
        acc_scr[...] = chunk()

    @pl.when(jnp.logical_and(j > 0, j < last))
    def _():
        acc_scr[...] += chunk()

    @pl.when(j == last)
    def _():
        _mix_epilogue(acc_scr[...] + chunk(), x_ref, gate_ref, pg_ref, o_ref)


def _ffn(x2d, mod3, layer, rowfn_maker, post_g, w_gate, w_up, w_down):
    rows = x2d.shape[0]
    tm = TM
    rowfn = rowfn_maker(tm)
    row_spec = pl.BlockSpec((tm, D_MODEL), lambda i, j: (i, 0))
    return pl.pallas_call(
        _ffn_kernel,
        grid=(rows // tm, D_FF // TF),
        in_specs=[row_spec, _mod_spec(layer, 3, rowfn), _mod_spec(layer, 4, rowfn),
                  _mod_spec(layer, 5, rowfn), pl.BlockSpec((1, D_MODEL), lambda i, j: (0, 0)),
                  pl.BlockSpec((1, D_MODEL, TF), lambda i, j: (layer, 0, j)),
                  pl.BlockSpec((1, D_MODEL, TF), lambda i, j: (layer, 0, j)),
                  pl.BlockSpec((1, TF, D_MODEL), lambda i, j: (layer, j, 0))],
        out_specs=row_spec,
        out_shape=jax.ShapeDtypeStruct((rows, D_MODEL), F32),
        scratch_shapes=[pltpu.VMEM((tm, D_MODEL), BF16), pltpu.VMEM((tm, D_MODEL), F32)],
        compiler_params=_cparams("parallel", "arbitrary"),
        name="ffn_%d" % rows,
    )(x2d, mod3, mod3, mod3, post_g.reshape(1, D_MODEL), w_gate, w_up, w_down)


def _inproj_cd_kernel(x_ref, sh_ref, sc_ref, w_ref, y_ref, f_ref):
    h = _rms(x_ref[...]) * (1.0 + sc_ref[0]) + sh_ref[0]
    hb = h.astype(BF16)

    def section(s):
        return jnp.dot(hb, w_ref[:, s * SEC:(s + 1) * SEC], preferred_element_type=F32)

    y_ref[...] = section(0) * jax.nn.sigmoid(section(1))
    f_ref[...] = section(2).astype(BF16)


def _inproj_cd(x2d, mod3, layer, rowfn_maker, w_in):
    rows = x2d.shape[0]
    tm = TM
    rowfn = rowfn_maker(tm)
    sec_spec = pl.BlockSpec((tm, SEC), lambda i: (i, 0))
    return pl.pallas_call(
        _inproj_cd_kernel,
        grid=(rows // tm,),
        in_specs=[pl.BlockSpec((tm, D_MODEL), lambda i: (i, 0)), _mod_spec(layer, 0, rowfn),
                  _mod_spec(layer, 1, rowfn), _resident()],
        out_specs=[sec_spec, sec_spec],
        out_shape=[jax.ShapeDtypeStruct((rows, SEC), F32), jax.ShapeDtypeStruct((rows, SEC), BF16)],
        compiler_params=_cparams("parallel"),
        name="inproj_cd",
    )(x2d, mod3, mod3, w_in)


def _conv_kernel(y_ref, prev_ref, next_ref, w_ref, b_ref, g_ref, be_ref, o_ref, ybuf, cbuf):
    i = pl.program_id(1)
    t = y_ref.shape[1]
    zeros = jnp.zeros((HALO, SEC), F32)
    ybuf[0:HALO, :] = jnp.where(i > 0, prev_ref[0], zeros)
    ybuf[HALO:HALO + t, :] = y_ref[0]
    ybuf[HALO + t:HALO + t + HALO, :] = jnp.where(i < pl.num_programs(1) - 1, next_ref[0], zeros)
    off = HALO - CONV_PAD

    def conv_block(rb, carry):
        r0 = pl.multiple_of(rb * CONV_RB, CONV_RB)
        for c in range(SEC // LANE):
            cs = slice(c * LANE, (c + 1) * LANE)
            win = ybuf[pl.ds(r0, CONV_RB + 2 * HALO), cs]
            acc = jnp.zeros((CONV_RB, LANE), F32) + b_ref[:, cs]
            n_win = CONV_RB + 2 * HALO
            for r in range(SUBLANE):
                shifted = win if r == 0 else pltpu.roll(win, n_win - r, 0)
                for a in range((2 * HALO) // SUBLANE):
                    w = a * SUBLANE + r - off
                    if 0 <= w < CONV_W:
                        acc = acc + (shifted[a * SUBLANE:a * SUBLANE + CONV_RB, :]
                                     * w_ref[w:w + 1, cs])
            cbuf[pl.ds(r0, CONV_RB), cs] = acc
        return carry

    lax.fori_loop(0, t // CONV_RB, conv_block, 0)

    def norm_block(rb, carry):
        r0 = pl.multiple_of(rb * CONV_RB, CONV_RB)
        v = cbuf[pl.ds(r0, CONV_RB), :]
        d = v - jnp.mean(v, axis=-1, keepdims=True)
        var = jnp.mean(d * d, axis=-1, keepdims=True)
        z = d * lax.rsqrt(var + EPS) * g_ref[...] + be_ref[...]
        o_ref[0, pl.ds(r0, CONV_RB), :] = jax.nn.silu(z).astype(BF16)
        return carry

    lax.fori_loop(0, t // CONV_RB, norm_block, 0, unroll=True)


def _conformer_conv(y3d, dw_w, dw_b, norm_g, norm_b):
    batch, n, _ = y3d.shape
    t = CONV_T
    hb = t // HALO
    nh = n // HALO
    vec = pl.BlockSpec((1, SEC), lambda b, i: (0, 0))
    return pl.pallas_call(
        _conv_kernel,
        grid=(batch, n // t),
        in_specs=[pl.BlockSpec((1, t, SEC), lambda b, i: (b, i, 0)),
                  pl.BlockSpec((1, HALO, SEC), lambda b, i: (b, jnp.maximum(i * hb - 1, 0), 0)),
                  pl.BlockSpec((1, HALO, SEC),
                               lambda b, i: (b, jnp.minimum((i + 1) * hb, nh - 1), 0)),
                  pl.BlockSpec((CONV_W, SEC), lambda b, i: (0, 0)), vec, vec, vec],
        out_specs=pl.BlockSpec((1, t, SEC), lambda b, i: (b, i, 0)),
        out_shape=jax.ShapeDtypeStruct((batch, n, SEC), BF16),
        scratch_shapes=[pltpu.VMEM((t + 2 * HALO, SEC), F32), pltpu.VMEM((t, SEC), F32)],
        compiler_params=_cparams("parallel", "arbitrary"),
        name="conformer_conv",
    )(y3d, y3d, y3d, dw_w, dw_b.reshape(1, SEC), norm_g.reshape(1, SEC), norm_b.reshape(1, SEC))


def _dft_constants():
    c = np.arange(LANE)
    ang_c = 2.0 * np.pi * np.outer(c, c) / LANE
    norm = 1.0 / math.sqrt(SEQ * LANE)
    chan = np.concatenate([np.cos(ang_c), -np.sin(ang_c)], axis=1) * norm
    r = np.arange(FFT_R)
    ang_r = 2.0 * np.pi * np.outer(r, r) / FFT_R
    cr, ci = np.cos(ang_r), -np.sin(ang_r)
    stage1 = np.block([[cr, -ci], [ci, cr]])
    eye = np.eye(FFT_KB)
    stage2_r, stage2_i = np.kron(cr, eye), np.kron(-ci, eye)
    ang_t = 2.0 * np.pi * np.outer(r, r) / SEQ
    return (jnp.asarray(chan, F32).astype(BF16), jnp.asarray(stage1, F32).astype(BF16),
            jnp.asarray(stage2_r, F32).astype(BF16), jnp.asarray(stage2_i, F32).astype(BF16),
            jnp.asarray(np.cos(ang_t), F32), jnp.asarray(-np.sin(ang_t), F32))


def _fft_a_kernel(x_ref, chan_ref, m1_ref, twc_ref, tws_ref, ar_ref, ai_ref):
    x = x_ref[0]
    nslab = x.shape[1] // LANE
    xs = jnp.concatenate([x[:, s * LANE:(s + 1) * LANE] for s in range(nslab)], axis=0)
    z = jnp.dot(xs, chan_ref[...], preferred_element_type=F32).astype(BF16)
    zr = jnp.concatenate([z[s * FFT_R:(s + 1) * FFT_R, 0:LANE] for s in range(nslab)], axis=1)
    zi = jnp.concatenate([z[s * FFT_R:(s + 1) * FFT_R, LANE:2 * LANE] for s in range(nslab)],
                         axis=1)
    a = jnp.dot(m1_ref[...], jnp.concatenate([zr, zi], axis=0), preferred_element_type=F32)
    a_r, a_i = a[0:FFT_R], a[FFT_R:2 * FFT_R]
    for j in range(x.shape[1] // SEC):
        cs = slice(j * SEC, (j + 1) * SEC)
        tc = jnp.concatenate([twc_ref[j]] * (SEC // LANE), axis=1)
        ts = jnp.concatenate([tws_ref[j]] * (SEC // LANE), axis=1)
        ar_ref[0, j] = (a_r[:, cs] * tc - a_i[:, cs] * ts).astype(BF16)
        ai_ref[0, j] = (a_r[:, cs] * ts + a_i[:, cs] * tc).astype(BF16)


def _fft_c_kernel(ar_ref, ai_ref, mr_ref, mi_ref, o_ref):
    rows = FFT_R * FFT_KB
    x_r = ar_ref[0].reshape(rows, SEC)
    x_i = ai_ref[0].reshape(rows, SEC)
    y = (jnp.dot(mr_ref[...], x_r, preferred_element_type=F32)
         + jnp.dot(mi_ref[...], x_i, preferred_element_type=F32))
    o_ref[0] = y.astype(BF16).reshape(FFT_R, FFT_KB, SEC)


def _fourier_mix(f3d):
    batch, n, _ = f3d.shape
    chan, m1, m2r, m2i, twc, tws = _dft_constants()
    twc = jnp.broadcast_to(twc[:, :, None], (FFT_R, FFT_R, LANE))
    tws = jnp.broadcast_to(tws[:, :, None], (FFT_R, FFT_R, LANE))
    nb = FFT_NB
    wide = nb * SEC
    xv = f3d.reshape(batch, FFT_R, FFT_R * SEC)
    blk = pl.BlockSpec((1, FFT_R, wide), lambda b, j: (b, 0, j))
    a_spec = pl.BlockSpec((1, nb, FFT_R, SEC), lambda b, j: (b, j, 0, 0))
    tw_spec = pl.BlockSpec((nb, FFT_R, LANE), lambda b, j: (j, 0, 0))
    a_shape = jax.ShapeDtypeStruct((batch, FFT_R, FFT_R, SEC), BF16)
    a_r, a_i = pl.pallas_call(
        _fft_a_kernel,
        grid=(batch, FFT_R // nb),
        in_specs=[blk, pl.BlockSpec((LANE, 2 * LANE), lambda b, j: (0, 0)),
                  pl.BlockSpec((2 * FFT_R, 2 * FFT_R), lambda b, j: (0, 0)), tw_spec, tw_spec],
        out_specs=[a_spec, a_spec],
        out_shape=[a_shape, a_shape],
        compiler_params=_cparams("parallel", "parallel"),
        name="fourier_stage_a",
    )(xv, chan, m1, twc, tws)
    c_spec = pl.BlockSpec((1, FFT_R, FFT_KB, SEC), lambda b, j: (b, 0, j, 0))
    m_spec = pl.BlockSpec((FFT_R * FFT_KB, FFT_R * FFT_KB), lambda b, j: (0, 0))
    out = pl.pallas_call(
        _fft_c_kernel,
        grid=(batch, FFT_R // FFT_KB),
        in_specs=[c_spec, c_spec, m_spec, m_spec],
        out_specs=c_spec,
        out_shape=a_shape,
        compiler_params=_cparams("parallel", "parallel"),
        name="fourier_stage_c",
    )(a_r, a_i, m2r, m2i)
    return out.reshape(batch, n, SEC)


def _rope_tables(n):
    rows = jnp.repeat(jnp.arange(n // GRID_W, dtype=F32), GRID_W)
    cols = jnp.tile(jnp.arange(GRID_W, dtype=F32), n // GRID_W)
    inv = ROPE_BASE ** (-jnp.arange(ROPE_FREQS, dtype=F32) / ROPE_FREQS)
    lane = np.arange(HEAD_W)
    freq = lane % ROPE_FREQS
    by_col = (lane % QK_DIM) // ROPE_HALF == 1
    second = (lane % ROPE_HALF) // ROPE_FREQS == 1
    ang = jnp.where(by_col[None, :], (cols[:, None] * inv)[:, freq], (rows[:, None] * inv)[:, freq])
    cos, sin = jnp.cos(ang), jnp.sin(ang)
    zero = jnp.zeros_like(sin)
    return cos, jnp.where(second[None, :], zero, -sin), jnp.where(second[None, :], sin, zero)


def kernel(x, c, ctx, c_ctx, mod_w, mod_b, post_mix_g, post_ffn_g, ffn_w_gate, ffn_w_up, ffn_w_down, ab_w_in, ab_w_out, ab_lam_q1, ab_lam_k1, ab_lam_q2, ab_lam_k2, ab_subln_g, ab_vnorm_g, ab_vnorm_b, ab_w_spatial, ab_b_spatial, cd_w_in, cd_w_out, cd_dw_w, cd_dw_b, cd_norm_g, cd_norm_b):
    batch, n, d = x.shape
    m = ctx.shape[1]
    depth = mod_w.shape[0]
    assert (n, d, m) == (SEQ, D_MODEL, CTX_LEN) and batch <= CTX_ROW

    cond = jnp.concatenate([c, c_ctx[None, :], jnp.zeros((N_COND - batch - 1, d), F32)], axis=0)
    mod3 = _modulation(cond, mod_w, mod_b).reshape(depth * N_COND * N_MOD, 1, d)
    rope_tabs = _rope_tables(n)
    stacks = [ffn_w_gate, ffn_w_up, ffn_w_down, ab_w_out, cd_w_in, cd_w_out]
    cast_later = [w.reshape(-1, w.shape[-1]) for w in stacks]

    x_lat = x.reshape(batch * n, d)
    x_ctx = ctx.reshape(batch * m, d)
    for l in range(depth):
        last = l == depth - 1
        even = l % 2 == 0
        i = l // 2
        use_ctx = (not last) or even
        if even:
            lam_init = 0.8 - 0.6 * math.exp(-0.3 * l)
            assert l == 0, "the weight-cast side job is attached to the first layer's attention"
            w_in = ab_w_in[i].astype(BF16)
            w_sp = ab_w_spatial[i].astype(BF16)
            b_sp = ab_b_spatial[i].reshape(SEC // LANE, CHUNK, 1)
            lam_tab = jnp.zeros((N_COND, LANE), F32).at[0:4, 0:QK_DIM].set(
                jnp.stack([ab_lam_q1[i], ab_lam_k1[i], ab_lam_q2[i], ab_lam_k2[i]]))
            q, k, v, u, vn = _inproj_ab(x_lat, mod3, l, _lat_row, w_in, ab_vnorm_g[i],
                                        ab_vnorm_b[i], rope_tabs)
            kc, vc = _inproj_ab(x_ctx, mod3, l, _ctx_row, w_in, ab_vnorm_g[i], ab_vnorm_b[i], None,
                                sections=(SEC_K, SEC_V), tag="_kv")
            qc, uc, vnc = _inproj_ab(x_ctx, mod3, l, _ctx_row, w_in, ab_vnorm_g[i], ab_vnorm_b[i],
                                     None, sections=(SEC_Q, SEC_U, SEC_VG), tag="_rest")
            q3, k3, v3 = (t.reshape(batch, n, SEC) for t in (q, k, v))
            kc3, vc3 = kc.reshape(batch, m, SEC), vc.reshape(batch, m, SEC)
            a_lat, cast = _diff_attention(q3, [k3, kc3], [v3, vc3], lam_tab, ab_subln_g[i], lam_init,
                                          to_cast=cast_later)
            wg, wu, wd, ab_w_out_b, cd_w_in_b, cd_w_out_b = (
                c2.reshape(w.shape) for c2, w in zip(cast, stacks))
            w_out = ab_w_out_b[i]
            x_lat_mix = _outproj_ab(a_lat.reshape(batch * n, SEC), u, vn, w_sp, b_sp, w_out, x_lat,
                                    mod3, l, _lat_row, post_mix_g[l])
            if not last:
                a_ctx, _ = _diff_attention(qc.reshape(batch, m, SEC), [kc3], [vc3], lam_tab,
                                           ab_subln_g[i], lam_init)
                x_ctx_mix = _outproj_ab(a_ctx.reshape(batch * m, SEC), uc, vnc, w_sp, b_sp, w_out,
                                        x_ctx, mod3, l, _ctx_row, post_mix_g[l])
        else:
            w_in, w_out = cd_w_in_b[i], cd_w_out_b[i]

            def mix_cd(x2d, rows_per_seq, rowfn_maker):
                y, f = _inproj_cd(x2d, mod3, l, rowfn_maker, w_in)
                nb = x2d.shape[0] // rows_per_seq
                yc = _conformer_conv(y.reshape(nb, rows_per_seq, SEC), cd_dw_w[i], cd_dw_b[i],
                                     cd_norm_g[i], cd_norm_b[i])
                yd = _fourier_mix(f.reshape(nb, rows_per_seq, SEC))
                return _outproj_cd(yc.reshape(-1, SEC), yd.reshape(-1, SEC), w_out, x2d, mod3, l,
                                   rowfn_maker, post_mix_g[l])

            x_lat_mix = mix_cd(x_lat, n, _lat_row)
            if use_ctx:
                raise NotImplementedError("odd non-final layers are outside this problem's depth")
        x_lat = _ffn(x_lat_mix, mod3, l, _lat_row, post_ffn_g[l], wg, wu, wd)
        if not last:
            x_ctx = _ffn(x_ctx_mix, mod3, l, _ctx_row, post_ffn_g[l], wg, wu, wd)
    return x_lat.reshape(batch, n, d)
```

```python
import functools
import math

import numpy as np
import jax
import jax.numpy as jnp
from jax import lax
from jax.experimental import pallas as pl
from jax.experimental.pallas import tpu as pltpu

F32 = jnp.float32
BF16 = jnp.bfloat16

D_MODEL = 2048
SEQ = 4096
GRID_W = 64
CTX_LEN = 256
EPS = 1e-6
N_MOD = 6
D_FF = 5632
HEADS = 8
QK_DIM = 64
HEAD_W = 128
SEC = 1024
CHUNK = 128
CONV_W = 31
CONV_PAD = (CONV_W - 1) // 2
ROPE_HALF = 32
ROPE_FREQS = 16
ROPE_BASE = 10000.0
FFT_R = 64
N_COND = 8
CTX_ROW = 4

LANE = 128
SUBLANE = 8
VMEM_LIMIT = 56 * 1024 * 1024
TM = 512
TF = 512
TQ = 256
CK = 256
CONV_T = 512
CONV_RB = 64
HALO = 16
FFT_KB = 16
CAST_SPLIT = 2
FFT_NB = 8


def _cparams(*sem):
    return pltpu.CompilerParams(dimension_semantics=sem, vmem_limit_bytes=VMEM_LIMIT)


def _resident():
    return pl.BlockSpec(memory_space=pltpu.VMEM)


def _mod_spec(layer, which, rowfn):
    return pl.BlockSpec((1, 1, D_MODEL),
                        lambda i, *_: ((layer * N_COND + rowfn(i)) * N_MOD + which, 0, 0))


def _lat_row(tm):
    return lambda i: i // (SEQ // tm)


def _ctx_row(_tm):
    return lambda i: CTX_ROW


def _divmod_nonneg(x, n):
    if n & (n - 1) == 0:
        return jnp.right_shift(x, n.bit_length() - 1), jnp.bitwise_and(x, n - 1)
    return x // n, x % n


def _rms(x):
    return x * lax.rsqrt(jnp.mean(x * x, axis=-1, keepdims=True) + EPS)


def _mod_kernel(c_ref, w_ref, b_ref, o_ref):
    s = jax.nn.silu(c_ref[...]).astype(BF16)
    o_ref[0] = jnp.dot(s, w_ref[0].astype(BF16), preferred_element_type=F32) + b_ref[0]


def _modulation(cond, mod_w, mod_b):
    depth, _, n = mod_w.shape
    tn = 1024
    return pl.pallas_call(
        _mod_kernel,
        grid=(depth, n // tn),
        in_specs=[pl.BlockSpec((N_COND, D_MODEL), lambda l, j: (0, 0)),
                  pl.BlockSpec((1, D_MODEL, tn), lambda l, j: (l, 0, j)),
                  pl.BlockSpec((1, 1, tn), lambda l, j: (l, 0, j))],
        out_specs=pl.BlockSpec((1, N_COND, tn), lambda l, j: (l, 0, j)),
        out_shape=jax.ShapeDtypeStruct((depth, N_COND, n), F32),
        compiler_params=_cparams("parallel", "parallel"),
        name="adaln_mod",
    )(cond, mod_w, mod_b.reshape(depth, 1, n))


SEC_Q, SEC_K, SEC_V, SEC_U, SEC_VG = range(5)


def _inproj_ab_kernel(*refs, rope, sections):
    n_in = 9 if rope else 6
    x_ref, sh_ref, sc_ref, w_ref, vg_ref, vb_ref = refs[:6]
    if rope:
        cos_ref, s1_ref, s2_ref = refs[6:9]
    out = dict(zip(sections, refs[n_in:]))
    h = _rms(x_ref[...]) * (1.0 + sc_ref[0]) + sh_ref[0]
    hb = h.astype(BF16)

    def section(s):
        return jnp.dot(hb, w_ref[:, s * SEC:(s + 1) * SEC], preferred_element_type=F32)

    def rotate(z):
        if not rope:
            return z
        cos, s1, s2 = cos_ref[...], s1_ref[...], s2_ref[...]
        outs = []
        for hd in range(HEADS):
            zs = z[:, hd * HEAD_W:(hd + 1) * HEAD_W]
            outs.append(zs * cos + pltpu.roll(zs, HEAD_W - ROPE_FREQS, 1) * s1
                        + pltpu.roll(zs, ROPE_FREQS, 1) * s2)
        return jnp.concatenate(outs, axis=1)

    if SEC_VG in out:
        g = jax.nn.gelu(section(SEC_VG))
        outs = []
        for gi in range(SEC // LANE):
            sl = slice(gi * LANE, (gi + 1) * LANE)
            gs = g[:, sl]
            d = gs - jnp.mean(gs, axis=-1, keepdims=True)
            var = jnp.mean(d * d, axis=-1, keepdims=True)
            outs.append(d * lax.rsqrt(var + EPS) * vg_ref[:, sl] + vb_ref[:, sl])
        out[SEC_VG][...] = jnp.concatenate(outs, axis=1).astype(BF16)
    if SEC_U in out:
        out[SEC_U][...] = jax.nn.gelu(section(SEC_U)).astype(BF16)
    if SEC_Q in out:
        out[SEC_Q][...] = (rotate(section(SEC_Q)) * (QK_DIM ** -0.5)).astype(BF16)
    if SEC_K in out:
        out[SEC_K][...] = rotate(section(SEC_K)).astype(BF16)
    if SEC_V in out:
        out[SEC_V][...] = section(SEC_V).astype(BF16)


def _inproj_ab(x2d, mod3, layer, rowfn_maker, w_in, vnorm_g, vnorm_b, rope_tabs,
               sections=(SEC_Q, SEC_K, SEC_V, SEC_U, SEC_VG), tag=""):
    rows = x2d.shape[0]
    tm = TM
    rowfn = rowfn_maker(tm)
    rope = rope_tabs is not None
    row_spec = pl.BlockSpec((tm, D_MODEL), lambda i: (i, 0))
    sec_spec = pl.BlockSpec((tm, SEC), lambda i: (i, 0))
    vec_spec = pl.BlockSpec((1, SEC), lambda i: (0, 0))
    in_specs = [row_spec, _mod_spec(layer, 0, rowfn), _mod_spec(layer, 1, rowfn), _resident(),
                vec_spec, vec_spec]
    args = [x2d, mod3, mod3, w_in, vnorm_g.reshape(1, SEC), vnorm_b.reshape(1, SEC)]
    if rope:
        tab_spec = pl.BlockSpec((tm, HEAD_W), lambda i: (i % (SEQ // tm), 0))
        in_specs += [tab_spec] * 3
        args += list(rope_tabs)
    out = jax.ShapeDtypeStruct((rows, SEC), BF16)
    return pl.pallas_call(
        functools.partial(_inproj_ab_kernel, rope=rope, sections=tuple(sections)),
        grid=(rows // tm,),
        in_specs=in_specs,
        out_specs=[sec_spec] * len(sections),
        out_shape=[out] * len(sections),
        compiler_params=_cparams("parallel"),
        name=("inproj_ab_lat" if rope else "inproj_ab_ctx") + tag,
    )(*args)


def _attn_kernel(*refs, seg_lens, tq, lam_init, n_cast, cast_period, nblk, n_blocks):
    nseg = len(seg_lens)
    q_ref = refs[0]
    k_refs = refs[1:1 + nseg]
    v_refs = refs[1 + nseg:1 + 2 * nseg]
    lam_ref, g_ref = refs[1 + 2 * nseg:3 + 2 * nseg]
    cast_in = refs[3 + 2 * nseg:3 + 2 * nseg + n_cast]
    o_ref = refs[3 + 2 * nseg + n_cast]
    cast_out = refs[4 + 2 * nseg + n_cast:4 + 2 * nseg + 2 * n_cast]
    vt_scr, s_even, s_odd, m_scr = refs[4 + 2 * nseg + 2 * n_cast:]

    t = pl.program_id(0)
    if n_cast:
        @pl.when(jnp.logical_and(t % cast_period == 0, t < n_blocks))
        def _():
            for src, dst in zip(cast_in, cast_out):
                dst[...] = src[...].astype(BF16)

    chunks = []
    base = 0
    for seg, n_keys in enumerate(seg_lens):
        for c0 in range(0, n_keys, CK):
            chunks.append((seg, c0, base + c0, min(CK, n_keys - c0)))
        base += n_keys

    def fold(x, op):
        return op(x.reshape(x.shape[0] // 8, 8, x.shape[1]), axis=0)

    def step(s_write, s_read):
        q_t = q_ref[0].astype(F32).T
        sub = lax.broadcasted_iota(jnp.int32, (HEAD_W, tq), 0)
        zero = jnp.zeros_like(q_t)
        qq_t = jnp.concatenate([jnp.where(sub < QK_DIM, q_t, zero),
                                jnp.where(sub >= QK_DIM, q_t, zero)], axis=1).astype(BF16)
        if s_read is not None:
            m = m_scr[0:1, :]
            lsum = jnp.zeros((8, 2 * tq), F32)
            acc = jnp.zeros((HEAD_W, 2 * tq), F32)
        mx = None
        for seg, c0, r0, w in chunks:
            s = jnp.dot(k_refs[seg][0, c0:c0 + w, :], qq_t, preferred_element_type=F32)
            s_write[r0:r0 + w, :] = s
            t = fold(s, jnp.max)
            mx = t if mx is None else jnp.maximum(mx, t)
            if s_read is not None:
                p = jnp.exp(s_read[r0:r0 + w, :] - m)
                lsum = lsum + fold(p, jnp.sum)
                acc = acc + jnp.dot(vt_scr[:, r0:r0 + w], p.astype(BF16),
                                    preferred_element_type=F32)
        if s_read is not None:
            o_all = acc / jnp.sum(lsum, axis=0, keepdims=True)
            lam_t = lam_ref[...]
            e1 = jnp.exp(jnp.sum(lam_t[0:1, :] * lam_t[1:2, :], axis=-1, keepdims=True))
            e2 = jnp.exp(jnp.sum(lam_t[2:3, :] * lam_t[3:4, :], axis=-1, keepdims=True))
            lam = e1 - e2 + lam_init
            o_t = o_all[:, :tq] - lam * o_all[:, tq:]
            o_t = o_t * lax.rsqrt(jnp.mean(o_t * o_t, axis=0, keepdims=True) + EPS)
            o_t = o_t * g_ref[...] * (1.0 - lam_init)
            o_ref[0] = o_t.T.astype(BF16)
        m_scr[...] = jnp.broadcast_to(jnp.max(mx, axis=0, keepdims=True), m_scr.shape)

    @pl.when(jnp.logical_and(t > 0, (t + nblk - 1) % nblk == 0))
    def _():
        for seg, c0, r0, w in chunks:
            vt_scr[:, r0:r0 + w] = v_refs[seg][0, c0:c0 + w, :].astype(F32).T.astype(BF16)

    @pl.when(t == 0)
    def _():
        step(s_even, None)

    @pl.when(jnp.logical_and(t > 0, t % 2 == 1))
    def _():
        step(s_odd, s_even)

    @pl.when(jnp.logical_and(t > 0, t % 2 == 0))
    def _():
        step(s_even, s_odd)


def _diff_attention(q, ks, vs, lam_tab, subln_g, lam_init, to_cast=()):
    batch, nq, _ = q.shape
    tq = min(TQ, nq)
    nblk = nq // tq
    n_blocks = batch * HEADS * nblk
    seg_lens = tuple(int(k.shape[1]) for k in ks)
    n_keys = sum(seg_lens)

    def where(blk):
        head, qb = _divmod_nonneg(blk, nblk)
        b, h = _divmod_nonneg(head, HEADS)
        return b, qb, h

    def scored(t):
        return jnp.minimum(t, n_blocks - 1)

    def finished(t):
        return jnp.maximum(t - 1, 0)

    q_spec = pl.BlockSpec((1, tq, HEAD_W), lambda t: where(scored(t)))
    o_spec = pl.BlockSpec((1, tq, HEAD_W), lambda t: where(finished(t)))

    def kv_spec(n, which):
        def index(t):
            b, _, h = where(which(t))
            return b, 0, h
        return pl.BlockSpec((1, n, HEAD_W), index)

    s_shape = pltpu.VMEM((n_keys, 2 * tq), F32)
    cast_period = -(-nblk // CAST_SPLIT)
    n_cast_blocks = -(-n_blocks // cast_period)
    cast_specs = []
    for w in to_cast:
        rows, cols = w.shape
        assert rows % (n_cast_blocks * 2 * SUBLANE) == 0, (rows, n_cast_blocks)
        cast_specs.append(pl.BlockSpec((rows // n_cast_blocks, cols),
                                       lambda t: (_divmod_nonneg(scored(t), cast_period)[0], 0)))
    const = lambda t: (0, 0)
    outs = pl.pallas_call(
        functools.partial(_attn_kernel, seg_lens=seg_lens, tq=tq, lam_init=lam_init,
                          n_cast=len(to_cast), cast_period=cast_period, nblk=nblk,
                          n_blocks=n_blocks),
        grid=(n_blocks + 1,),
        in_specs=[q_spec] + [kv_spec(n, scored) for n in seg_lens]
                 + [kv_spec(n, finished) for n in seg_lens]
                 + [pl.BlockSpec((N_COND, LANE), const), pl.BlockSpec((HEAD_W, 1), const)]
                 + cast_specs,
        out_specs=[o_spec] + cast_specs,
        out_shape=[jax.ShapeDtypeStruct(q.shape, BF16)]
                  + [jax.ShapeDtypeStruct(w.shape, BF16) for w in to_cast],
        scratch_shapes=[pltpu.VMEM((HEAD_W, n_keys), BF16), s_shape, s_shape,
                        pltpu.VMEM((8, 2 * tq), F32)],
        compiler_params=_cparams("arbitrary"),
        name="diff_attn_%d" % nq,
    )(q, *ks, *vs, lam_tab, subln_g.reshape(HEAD_W, 1), *to_cast)
    return outs[0], tuple(outs[1:])


def _mix_epilogue(y, x_ref, gate_ref, pg_ref, o_ref):
    o_ref[...] = x_ref[...] + gate_ref[0] * (_rms(y) * pg_ref[...])


def _outproj_ab_kernel(a_ref, u_ref, vn_ref, wsp_ref, bsp_ref, wo_ref, x_ref, gate_ref, pg_ref,
                       o_ref, s_scr, *, tm):
    y = jnp.dot(a_ref[...], wo_ref[0:SEC, :], preferred_element_type=F32)
    for c in range(tm // CHUNK):
        rs = slice(c * CHUNK, (c + 1) * CHUNK)
        for g in range(SEC // LANE):
            cs = slice(g * LANE, (g + 1) * LANE)
            sv = jnp.dot(wsp_ref[g], vn_ref[rs, cs], preferred_element_type=F32) + bsp_ref[g]
            s_scr[rs, cs] = (u_ref[rs, cs].astype(F32) * sv).astype(BF16)
    y = y + jnp.dot(s_scr[...], wo_ref[SEC:2 * SEC, :], preferred_element_type=F32)
    _mix_epilogue(y, x_ref, gate_ref, pg_ref, o_ref)


def _outproj_ab(a2d, u2d, vn2d, w_sp, b_sp, w_out, x2d, mod3, layer, rowfn_maker, post_g):
    rows = x2d.shape[0]
    tm = TM
    rowfn = rowfn_maker(tm)
    sec_spec = pl.BlockSpec((tm, SEC), lambda i: (i, 0))
    row_spec = pl.BlockSpec((tm, D_MODEL), lambda i: (i, 0))
    return pl.pallas_call(
        functools.partial(_outproj_ab_kernel, tm=tm),
        grid=(rows // tm,),
        in_specs=[sec_spec, sec_spec, sec_spec, _resident(), _resident(), _resident(), row_spec,
                  _mod_spec(layer, 2, rowfn), pl.BlockSpec((1, D_MODEL), lambda i: (0, 0))],
        out_specs=row_spec,
        out_shape=jax.ShapeDtypeStruct((rows, D_MODEL), F32),
        scratch_shapes=[pltpu.VMEM((tm, SEC), BF16)],
        compiler_params=_cparams("parallel"),
        name="outproj_ab_%d" % rows,
    )(a2d, u2d, vn2d, w_sp, b_sp, w_out, x2d, mod3, post_g.reshape(1, D_MODEL))


def _outproj_cd_kernel(yc_ref, yd_ref, wo_ref, x_ref, gate_ref, pg_ref, o_ref):
    y = (jnp.dot(yc_ref[...], wo_ref[0:SEC, :], preferred_element_type=F32)
         + jnp.dot(yd_ref[...], wo_ref[SEC:2 * SEC, :], preferred_element_type=F32))
    _mix_epilogue(y, x_ref, gate_ref, pg_ref, o_ref)


def _outproj_cd(yc2d, yd2d, w_out, x2d, mod3, layer, rowfn_maker, post_g):
    rows = x2d.shape[0]
    tm = TM
    rowfn = rowfn_maker(tm)
    sec_spec = pl.BlockSpec((tm, SEC), lambda i: (i, 0))
    row_spec = pl.BlockSpec((tm, D_MODEL), lambda i: (i, 0))
    return pl.pallas_call(
        _outproj_cd_kernel,
        grid=(rows // tm,),
        in_specs=[sec_spec, sec_spec, _resident(), row_spec, _mod_spec(layer, 2, rowfn),
                  pl.BlockSpec((1, D_MODEL), lambda i: (0, 0))],
        out_specs=row_spec,
        out_shape=jax.ShapeDtypeStruct((rows, D_MODEL), F32),
        compiler_params=_cparams("parallel"),
        name="outproj_cd",
    )(yc2d, yd2d, w_out, x2d, mod3, post_g.reshape(1, D_MODEL))


def _ffn_kernel(x_ref, sh_ref, sc_ref, gate_ref, pg_ref, wg_ref, wu_ref, wd_ref, o_ref,
                hb_scr, acc_scr):
    j = pl.program_id(1)
    last = pl.num_programs(1) - 1

    def chunk():
        hb = hb_scr[...]
        g = jnp.dot(hb, wg_ref[0], preferred_element_type=F32)
        u = jnp.dot(hb, wu_ref[0], preferred_element_type=F32)
        a = (jax.nn.silu(g) * u).astype(BF16)
        return jnp.dot(a, wd_ref[0], preferred_element_type=F32)

    @pl.when(j == 0)
    def _():
        h = _rms(x_ref[...]) * (1.0 + sc_ref[0]) + sh_ref[0]
        hb_scr[...] = h.astype(BF16)
        acc_scr[...] = chunk()

    @pl.when(jnp.logical_and(j > 0, j < last))
    def _():
        acc_scr[...] += chunk()

    @pl.when(j == last)
    def _():
        _mix_epilogue(acc_scr[...] + chunk(), x_ref, gate_ref, pg_ref, o_ref)


def _ffn(x2d, mod3, layer, rowfn_maker, post_g, w_gate, w_up, w_down):
    rows = x2d.shape[0]
    tm = TM
    rowfn = rowfn_maker(tm)
    row_spec = pl.BlockSpec((tm, D_MODEL), lambda i, j: (i, 0))
    return pl.pallas_call(
        _ffn_kernel,
        grid=(rows // tm, D_FF // TF),
        in_specs=[row_spec, _mod_spec(layer, 3, rowfn), _mod_spec(layer, 4, rowfn),
                  _mod_spec(layer, 5, rowfn), pl.BlockSpec((1, D_MODEL), lambda i, j: (0, 0)),
                  pl.BlockSpec((1, D_MODEL, TF), lambda i, j: (layer, 0, j)),
                  pl.BlockSpec((1, D_MODEL, TF), lambda i, j: (layer, 0, j)),
                  pl.BlockSpec((1, TF, D_MODEL), lambda i, j: (layer, j, 0))],
        out_specs=row_spec,
        out_shape=jax.ShapeDtypeStruct((rows, D_MODEL), F32),
        scratch_shapes=[pltpu.VMEM((tm, D_MODEL), BF16), pltpu.VMEM((tm, D_MODEL), F32)],
        compiler_params=_cparams("parallel", "arbitrary"),
        name="ffn_%d" % rows,
    )(x2d, mod3, mod3, mod3, post_g.reshape(1, D_MODEL), w_gate, w_up, w_down)


def _inproj_cd_kernel(x_ref, sh_ref, sc_ref, w_ref, y_ref, f_ref):
    h = _rms(x_ref[...]) * (1.0 + sc_ref[0]) + sh_ref[0]
    hb = h.astype(BF16)

    def section(s):
        return jnp.dot(hb, w_ref[:, s * SEC:(s + 1) * SEC], preferred_element_type=F32)

    y_ref[...] = section(0) * jax.nn.sigmoid(section(1))
    f_ref[...] = section(2).astype(BF16)


def _inproj_cd(x2d, mod3, layer, rowfn_maker, w_in):
    rows = x2d.shape[0]
    tm = TM
    rowfn = rowfn_maker(tm)
    sec_spec = pl.BlockSpec((tm, SEC), lambda i: (i, 0))
    return pl.pallas_call(
        _inproj_cd_kernel,
        grid=(rows // tm,),
        in_specs=[pl.BlockSpec((tm, D_MODEL), lambda i: (i, 0)), _mod_spec(layer, 0, rowfn),
                  _mod_spec(layer, 1, rowfn), _resident()],
        out_specs=[sec_spec, sec_spec],
        out_shape=[jax.ShapeDtypeStruct((rows, SEC), F32), jax.ShapeDtypeStruct((rows, SEC), BF16)],
        compiler_params=_cparams("parallel"),
        name="inproj_cd",
    )(x2d, mod3, mod3, w_in)


def _conv_kernel(y_ref, prev_ref, next_ref, w_ref, b_ref, g_ref, be_ref, o_ref, ybuf, cbuf):
    i = pl.program_id(1)
    t = y_ref.shape[1]
    zeros = jnp.zeros((HALO, SEC), F32)
    ybuf[0:HALO, :] = jnp.where(i > 0, prev_ref[0], zeros)
    ybuf[HALO:HALO + t, :] = y_ref[0]
    ybuf[HALO + t:HALO + t + HALO, :] = jnp.where(i < pl.num_programs(1) - 1, next_ref[0], zeros)
    off = HALO - CONV_PAD

    def conv_block(rb, carry):
        r0 = pl.multiple_of(rb * CONV_RB, CONV_RB)
        for c in range(SEC // LANE):
            cs = slice(c * LANE, (c + 1) * LANE)
            win = ybuf[pl.ds(r0, CONV_RB + 2 * HALO), cs]
            acc = jnp.zeros((CONV_RB, LANE), F32) + b_ref[:, cs]
            n_win = CONV_RB + 2 * HALO
            for r in range(SUBLANE):
                shifted = win if r == 0 else pltpu.roll(win, n_win - r, 0)
                for a in range((2 * HALO) // SUBLANE):
                    w = a * SUBLANE + r - off
                    if 0 <= w < CONV_W:
                        acc = acc + (shifted[a * SUBLANE:a * SUBLANE + CONV_RB, :]
                                     * w_ref[w:w + 1, cs])
            cbuf[pl.ds(r0, CONV_RB), cs] = acc
        return carry

    lax.fori_loop(0, t // CONV_RB, conv_block, 0)

    def norm_block(rb, carry):
        r0 = pl.multiple_of(rb * CONV_RB, CONV_RB)
        v = cbuf[pl.ds(r0, CONV_RB), :]
        d = v - jnp.mean(v, axis=-1, keepdims=True)
        var = jnp.mean(d * d, axis=-1, keepdims=True)
        z = d * lax.rsqrt(var + EPS) * g_ref[...] + be_ref[...]
        o_ref[0, pl.ds(r0, CONV_RB), :] = jax.nn.silu(z).astype(BF16)
        return carry

    lax.fori_loop(0, t // CONV_RB, norm_block, 0, unroll=True)


def _conformer_conv(y3d, dw_w, dw_b, norm_g, norm_b):
    batch, n, _ = y3d.shape
    t = CONV_T
    hb = t // HALO
    nh = n // HALO
    vec = pl.BlockSpec((1, SEC), lambda b, i: (0, 0))
    return pl.pallas_call(
        _conv_kernel,
        grid=(batch, n // t),
        in_specs=[pl.BlockSpec((1, t, SEC), lambda b, i: (b, i, 0)),
                  pl.BlockSpec((1, HALO, SEC), lambda b, i: (b, jnp.maximum(i * hb - 1, 0), 0)),
                  pl.BlockSpec((1, HALO, SEC),
                               lambda b, i: (b, jnp.minimum((i + 1) * hb, nh - 1), 0)),
                  pl.BlockSpec((CONV_W, SEC), lambda b, i: (0, 0)), vec, vec, vec],
        out_specs=pl.BlockSpec((1, t, SEC), lambda b, i: (b, i, 0)),
        out_shape=jax.ShapeDtypeStruct((batch, n, SEC), BF16),
        scratch_shapes=[pltpu.VMEM((t + 2 * HALO, SEC), F32), pltpu.VMEM((t, SEC), F32)],
        compiler_params=_cparams("parallel", "arbitrary"),
        name="conformer_conv",
    )(y3d, y3d, y3d, dw_w, dw_b.reshape(1, SEC), norm_g.reshape(1, SEC), norm_b.reshape(1, SEC))


def _dft_constants():
    c = np.arange(LANE)
    ang_c = 2.0 * np.pi * np.outer(c, c) / LANE
    norm = 1.0 / math.sqrt(SEQ * LANE)
    chan = np.concatenate([np.cos(ang_c), -np.sin(ang_c)], axis=1) * norm
    r = np.arange(FFT_R)
    ang_r = 2.0 * np.pi * np.outer(r, r) / FFT_R
    cr, ci = np.cos(ang_r), -np.sin(ang_r)
    stage1 = np.block([[cr, -ci], [ci, cr]])
    eye = np.eye(FFT_KB)
    stage2_r, stage2_i = np.kron(cr, eye), np.kron(-ci, eye)
    ang_t = 2.0 * np.pi * np.outer(r, r) / SEQ
    return (jnp.asarray(chan, F32).astype(BF16), jnp.asarray(stage1, F32).astype(BF16),
            jnp.asarray(stage2_r, F32).astype(BF16), jnp.asarray(stage2_i, F32).astype(BF16),
            jnp.asarray(np.cos(ang_t), F32), jnp.asarray(-np.sin(ang_t), F32))


def _fft_a_kernel(x_ref, chan_ref, m1_ref, twc_ref, tws_ref, ar_ref, ai_ref):
    x = x_ref[0]
    nslab = x.shape[1] // LANE
    xs = jnp.concatenate([x[:, s * LANE:(s + 1) * LANE] for s in range(nslab)], axis=0)
    z = jnp.dot(xs, chan_ref[...], preferred_element_type=F32).astype(BF16)
    zr = jnp.concatenate([z[s * FFT_R:(s + 1) * FFT_R, 0:LANE] for s in range(nslab)], axis=1)
    zi = jnp.concatenate([z[s * FFT_R:(s + 1) * FFT_R, LANE:2 * LANE] for s in range(nslab)],
                         axis=1)
    a = jnp.dot(m1_ref[...], jnp.concatenate([zr, zi], axis=0), preferred_element_type=F32)
    a_r, a_i = a[0:FFT_R], a[FFT_R:2 * FFT_R]
    for j in range(x.shape[1] // SEC):
        cs = slice(j * SEC, (j + 1) * SEC)
        tc = jnp.concatenate([twc_ref[j]] * (SEC // LANE), axis=1)
        ts = jnp.concatenate([tws_ref[j]] * (SEC // LANE), axis=1)
        ar_ref[0, j] = (a_r[:, cs] * tc - a_i[:, cs] * ts).astype(BF16)
        ai_ref[0, j] = (a_r[:, cs] * ts + a_i[:, cs] * tc).astype(BF16)


def _fft_c_kernel(ar_ref, ai_ref, mr_ref, mi_ref, o_ref):
    rows = FFT_R * FFT_KB
    x_r = ar_ref[0].reshape(rows, SEC)
    x_i = ai_ref[0].reshape(rows, SEC)
    y = (jnp.dot(mr_ref[...], x_r, preferred_element_type=F32)
         + jnp.dot(mi_ref[...], x_i, preferred_element_type=F32))
    o_ref[0] = y.astype(BF16).reshape(FFT_R, FFT_KB, SEC)


def _fourier_mix(f3d):
    batch, n, _ = f3d.shape
    chan, m1, m2r, m2i, twc, tws = _dft_constants()
    twc = jnp.broadcast_to(twc[:, :, None], (FFT_R, FFT_R, LANE))
    tws = jnp.broadcast_to(tws[:, :, None], (FFT_R, FFT_R, LANE))
    nb = FFT_NB
    wide = nb * SEC
    xv = f3d.reshape(batch, FFT_R, FFT_R * SEC)
    blk = pl.BlockSpec((1, FFT_R, wide), lambda b, j: (b, 0, j))
    a_spec = pl.BlockSpec((1, nb, FFT_R, SEC), lambda b, j: (b, j, 0, 0))
    tw_spec = pl.BlockSpec((nb, FFT_R, LANE), lambda b, j: (j, 0, 0))
    a_shape = jax.ShapeDtypeStruct((batch, FFT_R, FFT_R, SEC), BF16)
    a_r, a_i = pl.pallas_call(
        _fft_a_kernel,
        grid=(batch, FFT_R // nb),
        in_specs=[blk, pl.BlockSpec((LANE, 2 * LANE), lambda b, j: (0, 0)),
                  pl.BlockSpec((2 * FFT_R, 2 * FFT_R), lambda b, j: (0, 0)), tw_spec, tw_spec],
        out_specs=[a_spec, a_spec],
        out_shape=[a_shape, a_shape],
        compiler_params=_cparams("parallel", "parallel"),
        name="fourier_stage_a",
    )(xv, chan, m1, twc, tws)
    c_spec = pl.BlockSpec((1, FFT_R, FFT_KB, SEC), lambda b, j: (b, 0, j, 0))
    m_spec = pl.BlockSpec((FFT_R * FFT_KB, FFT_R * FFT_KB), lambda b, j: (0, 0))
    out = pl.pallas_call(
        _fft_c_kernel,
        grid=(batch, FFT_R // FFT_KB),
        in_specs=[c_spec, c_spec, m_spec, m_spec],
        out_specs=c_spec,
        out_shape=a_shape,
        compiler_params=_cparams("parallel", "parallel"),
        name="fourier_stage_c",
    )(a_r, a_i, m2r, m2i)
    return out.reshape(batch, n, SEC)


def _rope_tables(n):
    rows = jnp.repeat(jnp.arange(n // GRID_W, dtype=F32), GRID_W)
    cols = jnp.tile(jnp.arange(GRID_W, dtype=F32), n // GRID_W)
    inv = ROPE_BASE ** (-jnp.arange(ROPE_FREQS, dtype=F32) / ROPE_FREQS)
    lane = np.arange(HEAD_W)
    freq = lane % ROPE_FREQS
    by_col = (lane % QK_DIM) // ROPE_HALF == 1
    second = (lane % ROPE_HALF) // ROPE_FREQS == 1
    ang = jnp.where(by_col[None, :], (cols[:, None] * inv)[:, freq], (rows[:, None] * inv)[:, freq])
    cos, sin = jnp.cos(ang), jnp.sin(ang)
    zero = jnp.zeros_like(sin)
    return cos, jnp.where(second[None, :], zero, -sin), jnp.where(second[None, :], sin, zero)


def kernel(x, c, ctx, c_ctx, mod_w, mod_b, post_mix_g, post_ffn_g, ffn_w_gate, ffn_w_up, ffn_w_down, ab_w_in, ab_w_out, ab_lam_q1, ab_lam_k1, ab_lam_q2, ab_lam_k2, ab_subln_g, ab_vnorm_g, ab_vnorm_b, ab_w_spatial, ab_b_spatial, cd_w_in, cd_w_out, cd_dw_w, cd_dw_b, cd_norm_g, cd_norm_b):
    batch, n, d = x.shape
    m = ctx.shape[1]
    depth = mod_w.shape[0]
    assert (n, d, m) == (SEQ, D_MODEL, CTX_LEN) and batch <= CTX_ROW

    cond = jnp.concatenate([c, c_ctx[None, :], jnp.zeros((N_COND - batch - 1, d), F32)], axis=0)
    mod3 = _modulation(cond, mod_w, mod_b).reshape(depth * N_COND * N_MOD, 1, d)
    rope_tabs = _rope_tables(n)
    stacks = [ffn_w_gate, ffn_w_up, ffn_w_down, ab_w_out, cd_w_in, cd_w_out]
    cast_later = [w.reshape(-1, w.shape[-1]) for w in stacks]

    x_lat = x.reshape(batch * n, d)
    x_ctx = ctx.reshape(batch * m, d)
    for l in range(depth):
        last = l == depth - 1
        even = l % 2 == 0
        i = l // 2
        use_ctx = (not last) or even
        if even:
            lam_init = 0.8 - 0.6 * math.exp(-0.3 * l)
            assert l == 0, "the weight-cast side job is attached to the first layer's attention"
            w_in = ab_w_in[i].astype(BF16)
            w_sp = ab_w_spatial[i].astype(BF16)
            b_sp = ab_b_spatial[i].reshape(SEC // LANE, CHUNK, 1)
            lam_tab = jnp.zeros((N_COND, LANE), F32).at[0:4, 0:QK_DIM].set(
                jnp.stack([ab_lam_q1[i], ab_lam_k1[i], ab_lam_q2[i], ab_lam_k2[i]]))
            q, k, v, u, vn = _inproj_ab(x_lat, mod3, l, _lat_row, w_in, ab_vnorm_g[i],
                                        ab_vnorm_b[i], rope_tabs)
            kc, vc = _inproj_ab(x_ctx, mod3, l, _ctx_row, w_in, ab_vnorm_g[i], ab_vnorm_b[i], None,
                                sections=(SEC_K, SEC_V), tag="_kv")
            qc, uc, vnc = _inproj_ab(x_ctx, mod3, l, _ctx_row, w_in, ab_vnorm_g[i], ab_vnorm_b[i],
                                     None, sections=(SEC_Q, SEC_U, SEC_VG), tag="_rest")
            q3, k3, v3 = (t.reshape(batch, n, SEC) for t in (q, k, v))
            kc3, vc3 = kc.reshape(batch, m, SEC), vc.reshape(batch, m, SEC)
            a_lat, cast = _diff_attention(q3, [k3, kc3], [v3, vc3], lam_tab, ab_subln_g[i], lam_init,
                                          to_cast=cast_later)
            wg, wu, wd, ab_w_out_b, cd_w_in_b, cd_w_out_b = (
                c2.reshape(w.shape) for c2, w in zip(cast, stacks))
            w_out = ab_w_out_b[i]
            x_lat_mix = _outproj_ab(a_lat.reshape(batch * n, SEC), u, vn, w_sp, b_sp, w_out, x_lat,
                                    mod3, l, _lat_row, post_mix_g[l])
            if not last:
                a_ctx, _ = _diff_attention(qc.reshape(batch, m, SEC), [kc3], [vc3], lam_tab,
                                           ab_subln_g[i], lam_init)
                x_ctx_mix = _outproj_ab(a_ctx.reshape(batch * m, SEC), uc, vnc, w_sp, b_sp, w_out,
                                        x_ctx, mod3, l, _ctx_row, post_mix_g[l])
        else:
            w_in, w_out = cd_w_in_b[i], cd_w_out_b[i]

            def mix_cd(x2d, rows_per_seq, rowfn_maker):
                y, f = _inproj_cd(x2d, mod3, l, rowfn_maker, w_in)
                nb = x2d.shape[0] // rows_per_seq
                yc = _conformer_conv(y.reshape(nb, rows_per_seq, SEC), cd_dw_w[i], cd_dw_b[i],
                                     cd_norm_g[i], cd_norm_b[i])
                yd = _fourier_mix(f.reshape(nb, rows_per_seq, SEC))
                return _outproj_cd(yc.reshape(-1, SEC), yd.reshape(-1, SEC), w_out, x2d, mod3, l,
                                   rowfn_maker, post_mix_g[l])

            x_lat_mix = mix_cd(x_lat, n, _lat_row)
            if use_ctx:
                raise NotImplementedError("odd non-final layers are outside this problem's depth")
        x_lat = _ffn(x_lat_mix, mod3, l, _lat_row, post_ffn_g[l], wg, wu, wd)
        if not last:
            x_ctx = _ffn(x_ctx_mix, mod3, l, _ctx_row, post_ffn_g[l], wg, wu, wd)
    return x_lat.reshape(batch, n, d)
```

```python
import functools
import math

import numpy as np
import jax
import jax.numpy as jnp
from jax import lax
from jax.experimental import pallas as pl
from jax.experimental.pallas import tpu as pltpu

F32 = jnp.float32
BF16 = jnp.bfloat16

D_MODEL = 2048
SEQ = 4096
GRID_W = 64
CTX_LEN = 256
EPS = 1e-6
N_MOD = 6
D_FF = 5632
HEADS = 8
QK_DIM = 64
HEAD_W = 128
SEC = 1024
CHUNK = 128
CONV_W = 31
CONV_PAD = (CONV_W - 1) // 2
ROPE_HALF = 32
ROPE_FREQS = 16
ROPE_BASE = 10000.0
FFT_R = 64
N_COND = 8
CTX_ROW = 4

LANE = 128
SUBLANE = 8
VMEM_LIMIT = 56 * 1024 * 1024
TM = 512
TF = 512
TQ = 256
CK = 256
CONV_T = 512
CONV_RB = 64
HALO = 16
FFT_KB = 16
CAST_SPLIT = 2
FFT_NB = 8


def _cparams(*sem):
    return pltpu.CompilerParams(dimension_semantics=sem, vmem_limit_bytes=VMEM_LIMIT)


def _resident():
    return pl.BlockSpec(memory_space=pltpu.VMEM)


def _mod_spec(layer, which, rowfn):
    return pl.BlockSpec((1, 1, D_MODEL),
                        lambda i, *_: ((layer * N_COND + rowfn(i)) * N_MOD + which, 0, 0))


def _lat_row(tm):
    return lambda i: i // (SEQ // tm)


def _ctx_row(_tm):
    return lambda i: CTX_ROW


def _divmod_nonneg(x, n):
    if n & (n - 1) == 0:
        return jnp.right_shift(x, n.bit_length() - 1), jnp.bitwise_and(x, n - 1)
    return x // n, x % n


def _rms(x):
    return x * lax.rsqrt(jnp.mean(x * x, axis=-1, keepdims=True) + EPS)


def _mod_kernel(c_ref, w_ref, b_ref, o_ref):
    s = jax.nn.silu(c_ref[...]).astype(BF16)
    o_ref[0] = jnp.dot(s, w_ref[0].astype(BF16), preferred_element_type=F32) + b_ref[0]


def _modulation(cond, mod_w, mod_b):
    depth, _, n = mod_w.shape
    tn = 1024
    return pl.pallas_call(
        _mod_kernel,
        grid=(depth, n // tn),
        in_specs=[pl.BlockSpec((N_COND, D_MODEL), lambda l, j: (0, 0)),
                  pl.BlockSpec((1, D_MODEL, tn), lambda l, j: (l, 0, j)),
                  pl.BlockSpec((1, 1, tn), lambda l, j: (l, 0, j))],
        out_specs=pl.BlockSpec((1, N_COND, tn), lambda l, j: (l, 0, j)),
        out_shape=jax.ShapeDtypeStruct((depth, N_COND, n), F32),
        compiler_params=_cparams("parallel", "parallel"),
        name="adaln_mod",
    )(cond, mod_w, mod_b.reshape(depth, 1, n))


SEC_Q, SEC_K, SEC_V, SEC_U, SEC_VG = range(5)


def _inproj_ab_kernel(*refs, rope, sections):
    n_in = 9 if rope else 6
    x_ref, sh_ref, sc_ref, w_ref, vg_ref, vb_ref = refs[:6]
    if rope:
        cos_ref, s1_ref, s2_ref = refs[6:9]
    out = dict(zip(sections, refs[n_in:]))
    h = _rms(x_ref[...]) * (1.0 + sc_ref[0]) + sh_ref[0]
    hb = h.astype(BF16)

    def section(s):
        return jnp.dot(hb, w_ref[:, s * SEC:(s + 1) * SEC], preferred_element_type=F32)

    def rotate(z):
        if not rope:
            return z
        cos, s1, s2 = cos_ref[...], s1_ref[...], s2_ref[...]
        outs = []
        for hd in range(HEADS):
            zs = z[:, hd * HEAD_W:(hd + 1) * HEAD_W]
            outs.append(zs * cos + pltpu.roll(zs, HEAD_W - ROPE_FREQS, 1) * s1
                        + pltpu.roll(zs, ROPE_FREQS, 1) * s2)
        return jnp.concatenate(outs, axis=1)

    if SEC_VG in out:
        g = jax.nn.gelu(section(SEC_VG))
        outs = []
        for gi in range(SEC // LANE):
            sl = slice(gi * LANE, (gi + 1) * LANE)
            gs = g[:, sl]
            d = gs - jnp.mean(gs, axis=-1, keepdims=True)
            var = jnp.mean(d * d, axis=-1, keepdims=True)
            outs.append(d * lax.rsqrt(var + EPS) * vg_ref[:, sl] + vb_ref[:, sl])
        out[SEC_VG][...] = jnp.concatenate(outs, axis=1).astype(BF16)
    if SEC_U in out:
        out[SEC_U][...] = jax.nn.gelu(section(SEC_U)).astype(BF16)
    if SEC_Q in out:
        out[SEC_Q][...] = (rotate(section(SEC_Q)) * (QK_DIM ** -0.5)).astype(BF16)
    if SEC_K in out:
        out[SEC_K][...] = rotate(section(SEC_K)).astype(BF16)
    if SEC_V in out:
        out[SEC_V][...] = section(SEC_V).astype(BF16)


def _inproj_ab(x2d, mod3, layer, rowfn_maker, w_in, vnorm_g, vnorm_b, rope_tabs,
               sections=(SEC_Q, SEC_K, SEC_V, SEC_U, SEC_VG), tag=""):
    rows = x2d.shape[0]
    tm = TM
    rowfn = rowfn_maker(tm)
    rope = rope_tabs is not None
    row_spec = pl.BlockSpec((tm, D_MODEL), lambda i: (i, 0))
    sec_spec = pl.BlockSpec((tm, SEC), lambda i: (i, 0))
    vec_spec = pl.BlockSpec((1, SEC), lambda i: (0, 0))
    in_specs = [row_spec, _mod_spec(layer, 0, rowfn), _mod_spec(layer, 1, rowfn), _resident(),
                vec_spec, vec_spec]
    args = [x2d, mod3, mod3, w_in, vnorm_g.reshape(1, SEC), vnorm_b.reshape(1, SEC)]
    if rope:
        tab_spec = pl.BlockSpec((tm, HEAD_W), lambda i: (i % (SEQ // tm), 0))
        in_specs += [tab_spec] * 3
        args += list(rope_tabs)
    out = jax.ShapeDtypeStruct((rows, SEC), BF16)
    return pl.pallas_call(
        functools.partial(_inproj_ab_kernel, rope=rope, sections=tuple(sections)),
        grid=(rows // tm,),
        in_specs=in_specs,
        out_specs=[sec_spec] * len(sections),
        out_shape=[out] * len(sections),
        compiler_params=_cparams("parallel"),
        name=("inproj_ab_lat" if rope else "inproj_ab_ctx") + tag,
    )(*args)


def _attn_kernel(*refs, seg_lens, tq, lam_init, n_cast, cast_period, nblk, n_blocks):
    nseg = len(seg_lens)
    q_ref = refs[0]
    k_refs = refs[1:1 + nseg]
    v_refs = refs[1 + nseg:1 + 2 * nseg]
    lam_ref, g_ref = refs[1 + 2 * nseg:3 + 2 * nseg]
    cast_in = refs[3 + 2 * nseg:3 + 2 * nseg + n_cast]
    o_ref = refs[3 + 2 * nseg + n_cast]
    cast_out = refs[4 + 2 * nseg + n_cast:4 + 2 * nseg + 2 * n_cast]
    vt_scr, s_even, s_odd, m_scr = refs[4 + 2 * nseg + 2 * n_cast:]

    t = pl.program_id(0)
    if n_cast:
        @pl.when(jnp.logical_and(t % cast_period == 0, t < n_blocks))
        def _():
            for src, dst in zip(cast_in, cast_out):
                dst[...] = src[...].astype(BF16)

    chunks = []
    base = 0
    for seg, n_keys in enumerate(seg_lens):
        for c0 in range(0, n_keys, CK):
            chunks.append((seg, c0, base + c0, min(CK, n_keys - c0)))
        base += n_keys

    def fold(x, op):
        return op(x.reshape(x.shape[0] // 8, 8, x.shape[1]), axis=0)

    def step(s_write, s_read):
        q_t = q_ref[0].astype(F32).T
        sub = lax.broadcasted_iota(jnp.int32, (HEAD_W, tq), 0)
        zero = jnp.zeros_like(q_t)
        qq_t = jnp.concatenate([jnp.where(sub < QK_DIM, q_t, zero),
                                jnp.where(sub >= QK_DIM, q_t, zero)], axis=1).astype(BF16)
        if s_read is not None:
            m = m_scr[0:1, :]
            lsum = jnp.zeros((8, 2 * tq), F32)
            acc = jnp.zeros((HEAD_W, 2 * tq), F32)
        mx = None
        for seg, c0, r0, w in chunks:
            s = jnp.dot(k_refs[seg][0, c0:c0 + w, :], qq_t, preferred_element_type=F32)
            s_write[r0:r0 + w, :] = s
            t = fold(s, jnp.max)
            mx = t if mx is None else jnp.maximum(mx, t)
            if s_read is not None:
                p = jnp.exp(s_read[r0:r0 + w, :] - m)
                lsum = lsum + fold(p, jnp.sum)
                acc = acc + jnp.dot(vt_scr[:, r0:r0 + w], p.astype(BF16),
                                    preferred_element_type=F32)
        if s_read is not None:
            o_all = acc / jnp.sum(lsum, axis=0, keepdims=True)
            lam_t = lam_ref[...]
            e1 = jnp.exp(jnp.sum(lam_t[0:1, :] * lam_t[1:2, :], axis=-1, keepdims=True))
            e2 = jnp.exp(jnp.sum(lam_t[2:3, :] * lam_t[3:4, :], axis=-1, keepdims=True))
            lam = e1 - e2 + lam_init
            o_t = o_all[:, :tq] - lam * o_all[:, tq:]
            o_t = o_t * lax.rsqrt(jnp.mean(o_t * o_t, axis=0, keepdims=True) + EPS)
            o_t = o_t * g_ref[...] * (1.0 - lam_init)
            o_ref[0] = o_t.T.astype(BF16)
        m_scr[...] = jnp.broadcast_to(jnp.max(mx, axis=0, keepdims=True), m_scr.shape)

    @pl.when(jnp.logical_and(t > 0, (t + nblk - 1) % nblk == 0))
    def _():
        for seg, c0, r0, w in chunks:
            vt_scr[:, r0:r0 + w] = v_refs[seg][0, c0:c0 + w, :].astype(F32).T.astype(BF16)

    @pl.when(t == 0)
    def _():
        step(s_even, None)

    @pl.when(jnp.logical_and(t > 0, t % 2 == 1))
    def _():
        step(s_odd, s_even)

    @pl.when(jnp.logical_and(t > 0, t % 2 == 0))
    def _():
        step(s_even, s_odd)


def _diff_attention(q, ks, vs, lam_tab, subln_g, lam_init, to_cast=()):
    batch, nq, _ = q.shape
    tq = min(TQ, nq)
    nblk = nq // tq
    n_blocks = batch * HEADS * nblk
    seg_lens = tuple(int(k.shape[1]) for k in ks)
    n_keys = sum(seg_lens)

    def where(blk):
        head, qb = _divmod_nonneg(blk, nblk)
        b, h = _divmod_nonneg(head, HEADS)
        return b, qb, h

    def scored(t):
        return jnp.minimum(t, n_blocks - 1)

    def finished(t):
        return jnp.maximum(t - 1, 0)

    q_spec = pl.BlockSpec((1, tq, HEAD_W), lambda t: where(scored(t)))
    o_spec = pl.BlockSpec((1, tq, HEAD_W), lambda t: where(finished(t)))

    def kv_spec(n, which):
        def index(t):
            b, _, h = where(which(t))
            return b, 0, h
        return pl.BlockSpec((1, n, HEAD_W), index)

    s_shape = pltpu.VMEM((n_keys, 2 * tq), F32)
    cast_period = -(-nblk // CAST_SPLIT)
    n_cast_blocks = -(-n_blocks // cast_period)
    cast_specs = []
    for w in to_cast:
        rows, cols = w.shape
        assert rows % (n_cast_blocks * 2 * SUBLANE) == 0, (rows, n_cast_blocks)
        cast_specs.append(pl.BlockSpec((rows // n_cast_blocks, cols),
                                       lambda t: (_divmod_nonneg(scored(t), cast_period)[0], 0)))
    const = lambda t: (0, 0)
    outs = pl.pallas_call(
        functools.partial(_attn_kernel, seg_lens=seg_lens, tq=tq, lam_init=lam_init,
                          n_cast=len(to_cast), cast_period=cast_period, nblk=nblk,
                          n_blocks=n_blocks),
        grid=(n_blocks + 1,),
        in_specs=[q_spec] + [kv_spec(n, scored) for n in seg_lens]
                 + [kv_spec(n, finished) for n in seg_lens]
                 + [pl.BlockSpec((N_COND, LANE), const), pl.BlockSpec((HEAD_W, 1), const)]
                 + cast_specs,
        out_specs=[o_spec] + cast_specs,
        out_shape=[jax.ShapeDtypeStruct(q.shape, BF16)]
                  + [jax.ShapeDtypeStruct(w.shape, BF16) for w in to_cast],
        scratch_shapes=[pltpu.VMEM((HEAD_W, n_keys), BF16), s_shape, s_shape,
                        pltpu.VMEM((8, 2 * tq), F32)],
        compiler_params=_cparams("arbitrary"),
        name="diff_attn_%d" % nq,
    )(q, *ks, *vs, lam_tab, subln_g.reshape(HEAD_W, 1), *to_cast)
    return outs[0], tuple(outs[1:])


def _mix_epilogue(y, x_ref, gate_ref, pg_ref, o_ref):
    o_ref[...] = x_ref[...] + gate_ref[0] * (_rms(y) * pg_ref[...])


def _outproj_ab_kernel(a_ref, u_ref, vn_ref, wsp_ref, bsp_ref, wo_ref, x_ref, gate_ref, pg_ref,
                       o_ref, s_scr, *, tm):
    y = jnp.dot(a_ref[...], wo_ref[0:SEC, :], preferred_element_type=F32)
    for c in range(tm // CHUNK):
        rs = slice(c * CHUNK, (c + 1) * CHUNK)
        for g in range(SEC // LANE):
            cs = slice(g * LANE, (g + 1) * LANE)
            sv = jnp.dot(wsp_ref[g], vn_ref[rs, cs], preferred_element_type=F32) + bsp_ref[g]
            s_scr[rs, cs] = (u_ref[rs, cs].astype(F32) * sv).astype(BF16)
    y = y + jnp.dot(s_scr[...], wo_ref[SEC:2 * SEC, :], preferred_element_type=F32)
    _mix_epilogue(y, x_ref, gate_ref, pg_ref, o_ref)


def _outproj_ab(a2d, u2d, vn2d, w_sp, b_sp, w_out, x2d, mod3, layer, rowfn_maker, post_g):
    rows = x2d.shape[0]
    tm = TM
    rowfn = rowfn_maker(tm)
    sec_spec = pl.BlockSpec((tm, SEC), lambda i: (i, 0))
    row_spec = pl.BlockSpec((tm, D_MODEL), lambda i: (i, 0))
    return pl.pallas_call(
        functools.partial(_outproj_ab_kernel, tm=tm),
        grid=(rows // tm,),
        in_specs=[sec_spec, sec_spec, sec_spec, _resident(), _resident(), _resident(), row_spec,
                  _mod_spec(layer, 2, rowfn), pl.BlockSpec((1, D_MODEL), lambda i: (0, 0))],
        out_specs=row_spec,
        out_shape=jax.ShapeDtypeStruct((rows, D_MODEL), F32),
        scratch_shapes=[pltpu.VMEM((tm, SEC), BF16)],
        compiler_params=_cparams("parallel"),
        name="outproj_ab_%d" % rows,
    )(a2d, u2d, vn2d, w_sp, b_sp, w_out, x2d, mod3, post_g.reshape(1, D_MODEL))


def _outproj_cd_kernel(yc_ref, yd_ref, wo_ref, x_ref, gate_ref, pg_ref, o_ref):
    y = (jnp.dot(yc_ref[...], wo_ref[0:SEC, :], preferred_element_type=F32)
         + jnp.dot(yd_ref[...], wo_ref[SEC:2 * SEC, :], preferred_element_type=F32))
    _mix_epilogue(y, x_ref, gate_ref, pg_ref, o_ref)


def _outproj_cd(yc2d, yd2d, w_out, x2d, mod3, layer, rowfn_maker, post_g):
    rows = x2d.shape[0]
    tm = TM
    rowfn = rowfn_maker(tm)
    sec_spec = pl.BlockSpec((tm, SEC), lambda i: (i, 0))
    row_spec = pl.BlockSpec((tm, D_MODEL), lambda i: (i, 0))
    return pl.pallas_call(
        _outproj_cd_kernel,
        grid=(rows // tm,),
        in_specs=[sec_spec, sec_spec, _resident(), row_spec, _mod_spec(layer, 2, rowfn),
                  pl.BlockSpec((1, D_MODEL), lambda i: (0, 0))],
        out_specs=row_spec,
        out_shape=jax.ShapeDtypeStruct((rows, D_MODEL), F32),
        compiler_params=_cparams("parallel"),
        name="outproj_cd",
    )(yc2d, yd2d, w_out, x2d, mod3, post_g.reshape(1, D_MODEL))


def _ffn_kernel(x_ref, sh_ref, sc_ref, gate_ref, pg_ref, wg_ref, wu_ref, wd_ref, o_ref,
                hb_scr, acc_scr):
    j = pl.program_id(1)
    last = pl.num_programs(1) - 1

    def chunk():
        hb = hb_scr[...]
        g = jnp.dot(hb, wg_ref[0], preferred_element_type=F32)
        u = jnp.dot(hb, wu_ref[0], preferred_element_type=F32)
        a = (jax.nn.silu(g) * u).astype(BF16)
        return jnp.dot(a, wd_ref[0], preferred_element_type=F32)

    @pl.when(j == 0)
    def _():
        h = _rms(x_ref[...]) * (1.0 + sc_ref[0]) + sh_ref[0]
        hb_scr[...] = h.astype(BF16)
        acc_scr[...] = chunk()

    @pl.when(jnp.logical_and(j > 0, j < last))
    def _():
        acc_scr[...] += chunk()

    @pl.when(j == last)
    def _():
        _mix_epilogue(acc_scr[...] + chunk(), x_ref, gate_ref, pg_ref, o_ref)


def _ffn(x2d, mod3, layer, rowfn_maker, post_g, w_gate, w_up, w_down):
    rows = x2d.shape[0]
    tm = TM
    rowfn = rowfn_maker(tm)
    row_spec = pl.BlockSpec((tm, D_MODEL), lambda i, j: (i, 0))
    return pl.pallas_call(
        _ffn_kernel,
        grid=(rows // tm, D_FF // TF),
        in_specs=[row_spec, _mod_spec(layer, 3, rowfn), _mod_spec(layer, 4, rowfn),
                  _mod_spec(layer, 5, rowfn), pl.BlockSpec((1, D_MODEL), lambda i, j: (0, 0)),
                  pl.BlockSpec((1, D_MODEL, TF), lambda i, j: (layer, 0, j)),
                  pl.BlockSpec((1, D_MODEL, TF), lambda i, j: (layer, 0, j)),
                  pl.BlockSpec((1, TF, D_MODEL), lambda i, j: (layer, j, 0))],
        out_specs=row_spec,
        out_shape=jax.ShapeDtypeStruct((rows, D_MODEL), F32),
        scratch_shapes=[pltpu.VMEM((tm, D_MODEL), BF16), pltpu.VMEM((tm, D_MODEL), F32)],
        compiler_params=_cparams("parallel", "arbitrary"),
        name="ffn_%d" % rows,
    )(x2d, mod3, mod3, mod3, post_g.reshape(1, D_MODEL), w_gate, w_up, w_down)


def _inproj_cd_kernel(x_ref, sh_ref, sc_ref, w_ref, k1_ref, twc_ref, tws_ref,
                      y_ref, ar_ref, ai_ref):
    rows = FFT_R * FFT_NB
    x = x_ref[0].reshape(rows, D_MODEL)
    h = _rms(x) * (1.0 + sc_ref[0]) + sh_ref[0]
    hb = h.astype(BF16)

    def section(s):
        return jnp.dot(hb, w_ref[:, s * SEC:(s + 1) * SEC], preferred_element_type=F32)

    y = section(0) * jax.nn.sigmoid(section(1))
    y_ref[0] = y.reshape(FFT_R, FFT_NB, SEC)
    f = section(2).astype(BF16)
    a = jnp.dot(k1_ref[...], f, preferred_element_type=F32)
    a_r, a_i = a[0:rows], a[rows:2 * rows]
    ngrp = SEC // LANE
    tc = jnp.concatenate([twc_ref[...].reshape(rows, LANE)] * ngrp, axis=1)
    ts = jnp.concatenate([tws_ref[...].reshape(rows, LANE)] * ngrp, axis=1)
    ar_ref[0] = (a_r * tc - a_i * ts).astype(BF16).reshape(FFT_NB, FFT_R, SEC)
    ai_ref[0] = (a_r * ts + a_i * tc).astype(BF16).reshape(FFT_NB, FFT_R, SEC)


def _inproj_cd(x3d, mod3, layer, w_in):
    batch = x3d.shape[0]
    _, k1, _, _, twc, tws = _dft_constants()
    twc = jnp.broadcast_to(twc[:, :, None], (FFT_R, FFT_R, LANE))
    tws = jnp.broadcast_to(tws[:, :, None], (FFT_R, FFT_R, LANE))
    x4 = x3d.reshape(batch, FFT_R, FFT_R, D_MODEL)
    tile = lambda width: pl.BlockSpec((1, FFT_R, FFT_NB, width), lambda b, j: (b, 0, j, 0))
    a_spec = pl.BlockSpec((1, FFT_NB, FFT_R, SEC), lambda b, j: (b, j, 0, 0))
    tw_spec = pl.BlockSpec((FFT_NB, FFT_R, LANE), lambda b, j: (j, 0, 0))
    a_shape = jax.ShapeDtypeStruct((batch, FFT_R, FFT_R, SEC), BF16)
    rows = FFT_R * FFT_NB
    y, a_r, a_i = pl.pallas_call(
        _inproj_cd_kernel,
        grid=(batch, FFT_R // FFT_NB),
        in_specs=[tile(D_MODEL), _mod_spec(layer, 0, lambda b: b), _mod_spec(layer, 1, lambda b: b),
                  _resident(), pl.BlockSpec((2 * rows, rows), lambda b, j: (0, 0)),
                  tw_spec, tw_spec],
        out_specs=[tile(SEC), a_spec, a_spec],
        out_shape=[jax.ShapeDtypeStruct((batch, FFT_R, FFT_R, SEC), F32), a_shape, a_shape],
        compiler_params=_cparams("parallel", "parallel"),
        name="inproj_cd",
    )(x4, mod3, mod3, w_in, k1, twc, tws)
    return y.reshape(batch, SEQ, SEC), a_r, a_i


def _conv_kernel(y_ref, prev_ref, next_ref, w_ref, b_ref, g_ref, be_ref, o_ref, ybuf, cbuf):
    i = pl.program_id(1)
    t = y_ref.shape[1]
    zeros = jnp.zeros((HALO, SEC), F32)
    ybuf[0:HALO, :] = jnp.where(i > 0, prev_ref[0], zeros)
    ybuf[HALO:HALO + t, :] = y_ref[0]
    ybuf[HALO + t:HALO + t + HALO, :] = jnp.where(i < pl.num_programs(1) - 1, next_ref[0], zeros)
    off = HALO - CONV_PAD

    def conv_block(rb, carry):
        r0 = pl.multiple_of(rb * CONV_RB, CONV_RB)
        for c in range(SEC // LANE):
            cs = slice(c * LANE, (c + 1) * LANE)
            win = ybuf[pl.ds(r0, CONV_RB + 2 * HALO), cs]
            acc = jnp.zeros((CONV_RB, LANE), F32) + b_ref[:, cs]
            n_win = CONV_RB + 2 * HALO
            for r in range(SUBLANE):
                shifted = win if r == 0 else pltpu.roll(win, n_win - r, 0)
                for a in range((2 * HALO) // SUBLANE):
                    w = a * SUBLANE + r - off
                    if 0 <= w < CONV_W:
                        acc = acc + (shifted[a * SUBLANE:a * SUBLANE + CONV_RB, :]
                                     * w_ref[w:w + 1, cs])
            cbuf[pl.ds(r0, CONV_RB), cs] = acc
        return carry

    lax.fori_loop(0, t // CONV_RB, conv_block, 0)

    def norm_block(rb, carry):
        r0 = pl.multiple_of(rb * CONV_RB, CONV_RB)
        v = cbuf[pl.ds(r0, CONV_RB), :]
        d = v - jnp.mean(v, axis=-1, keepdims=True)
        var = jnp.mean(d * d, axis=-1, keepdims=True)
        z = d * lax.rsqrt(var + EPS) * g_ref[...] + be_ref[...]
        o_ref[0, pl.ds(r0, CONV_RB), :] = jax.nn.silu(z).astype(BF16)
        return carry

    lax.fori_loop(0, t // CONV_RB, norm_block, 0, unroll=True)


def _conformer_conv(y3d, dw_w, dw_b, norm_g, norm_b):
    batch, n, _ = y3d.shape
    t = CONV_T
    hb = t // HALO
    nh = n // HALO
    vec = pl.BlockSpec((1, SEC), lambda b, i: (0, 0))
    return pl.pallas_call(
        _conv_kernel,
        grid=(batch, n // t),
        in_specs=[pl.BlockSpec((1, t, SEC), lambda b, i: (b, i, 0)),
                  pl.BlockSpec((1, HALO, SEC), lambda b, i: (b, jnp.maximum(i * hb - 1, 0), 0)),
                  pl.BlockSpec((1, HALO, SEC),
                               lambda b, i: (b, jnp.minimum((i + 1) * hb, nh - 1), 0)),
                  pl.BlockSpec((CONV_W, SEC), lambda b, i: (0, 0)), vec, vec, vec],
        out_specs=pl.BlockSpec((1, t, SEC), lambda b, i: (b, i, 0)),
        out_shape=jax.ShapeDtypeStruct((batch, n, SEC), BF16),
        scratch_shapes=[pltpu.VMEM((t + 2 * HALO, SEC), F32), pltpu.VMEM((t, SEC), F32)],
        compiler_params=_cparams("parallel", "arbitrary"),
        name="conformer_conv",
    )(y3d, y3d, y3d, dw_w, dw_b.reshape(1, SEC), norm_g.reshape(1, SEC), norm_b.reshape(1, SEC))


def _dft_constants():
    c = np.arange(LANE)
    ang_c = 2.0 * np.pi * np.outer(c, c) / LANE
    norm = 1.0 / math.sqrt(SEQ * LANE)
    cc, sc = np.cos(ang_c) * norm, np.sin(ang_c) * norm
    chan = np.block([[cc, -sc], [sc, cc]])
    r = np.arange(FFT_R)
    ang_r = 2.0 * np.pi * np.outer(r, r) / FFT_R
    cr, ci = np.cos(ang_r), -np.sin(ang_r)
    eye_nb = np.eye(FFT_NB)
    stage1 = np.concatenate([np.einsum('kn,pq->pknq', m, eye_nb).reshape(
        FFT_NB * FFT_R, FFT_R * FFT_NB) for m in (cr, ci)], axis=0)
    eye = np.eye(FFT_KB)
    stage2_r, stage2_i = np.kron(cr, eye), np.kron(-ci, eye)
    ang_t = 2.0 * np.pi * np.outer(r, r) / SEQ
    return (jnp.asarray(chan, F32).astype(BF16), jnp.asarray(stage1, F32).astype(BF16),
            jnp.asarray(stage2_r, F32).astype(BF16), jnp.asarray(stage2_i, F32).astype(BF16),
            jnp.asarray(np.cos(ang_t), F32), jnp.asarray(-np.sin(ang_t), F32))


def _fft_c_kernel(ar_ref, ai_ref, chan_ref, mr_ref, mi_ref, o_ref):
    rows = FFT_R * FFT_KB
    ngrp = SEC // LANE
    x_r = ar_ref[0].reshape(rows, SEC)
    x_i = ai_ref[0].reshape(rows, SEC)
    lhs = jnp.concatenate(
        [jnp.concatenate([x_r[:, g * LANE:(g + 1) * LANE], x_i[:, g * LANE:(g + 1) * LANE]], axis=1)
         for g in range(ngrp)], axis=0)
    z = jnp.dot(lhs, chan_ref[...], preferred_element_type=F32).astype(BF16)
    z_r = jnp.concatenate([z[g * rows:(g + 1) * rows, 0:LANE] for g in range(ngrp)], axis=1)
    z_i = jnp.concatenate([z[g * rows:(g + 1) * rows, LANE:2 * LANE] for g in range(ngrp)], axis=1)
    y = (jnp.dot(mr_ref[...], z_r, preferred_element_type=F32)
         + jnp.dot(mi_ref[...], z_i, preferred_element_type=F32))
    o_ref[0] = y.astype(BF16).reshape(FFT_R, FFT_KB, SEC)


def _fourier_mix(a_r, a_i):
    batch = a_r.shape[0]
    chan, _, m2r, m2i, _, _ = _dft_constants()
    c_spec = pl.BlockSpec((1, FFT_R, FFT_KB, SEC), lambda b, j: (b, 0, j, 0))
    m_spec = pl.BlockSpec((FFT_R * FFT_KB, FFT_R * FFT_KB), lambda b, j: (0, 0))
    out = pl.pallas_call(
        _fft_c_kernel,
        grid=(batch, FFT_R // FFT_KB),
        in_specs=[c_spec, c_spec, pl.BlockSpec((2 * LANE, 2 * LANE), lambda b, j: (0, 0)),
                  m_spec, m_spec],
        out_specs=c_spec,
        out_shape=jax.ShapeDtypeStruct(a_r.shape, BF16),
        compiler_params=_cparams("parallel", "parallel"),
        name="fourier_stage_c",
    )(a_r, a_i, chan, m2r, m2i)
    return out.reshape(batch, SEQ, SEC)


def _rope_tables(n):
    rows = jnp.repeat(jnp.arange(n // GRID_W, dtype=F32), GRID_W)
    cols = jnp.tile(jnp.arange(GRID_W, dtype=F32), n // GRID_W)
    inv = ROPE_BASE ** (-jnp.arange(ROPE_FREQS, dtype=F32) / ROPE_FREQS)
    lane = np.arange(HEAD_W)
    freq = lane % ROPE_FREQS
    by_col = (lane % QK_DIM) // ROPE_HALF == 1
    second = (lane % ROPE_HALF) // ROPE_FREQS == 1
    ang = jnp.where(by_col[None, :], (cols[:, None] * inv)[:, freq], (rows[:, None] * inv)[:, freq])
    cos, sin = jnp.cos(ang), jnp.sin(ang)
    zero = jnp.zeros_like(sin)
    return cos, jnp.where(second[None, :], zero, -sin), jnp.where(second[None, :], sin, zero)


def kernel(x, c, ctx, c_ctx, mod_w, mod_b, post_mix_g, post_ffn_g, ffn_w_gate, ffn_w_up, ffn_w_down, ab_w_in, ab_w_out, ab_lam_q1, ab_lam_k1, ab_lam_q2, ab_lam_k2, ab_subln_g, ab_vnorm_g, ab_vnorm_b, ab_w_spatial, ab_b_spatial, cd_w_in, cd_w_out, cd_dw_w, cd_dw_b, cd_norm_g, cd_norm_b):
    batch, n, d = x.shape
    m = ctx.shape[1]
    depth = mod_w.shape[0]
    assert (n, d, m) == (SEQ, D_MODEL, CTX_LEN) and batch <= CTX_ROW

    cond = jnp.concatenate([c, c_ctx[None, :], jnp.zeros((N_COND - batch - 1, d), F32)], axis=0)
    mod3 = _modulation(cond, mod_w, mod_b).reshape(depth * N_COND * N_MOD, 1, d)
    rope_tabs = _rope_tables(n)
    stacks = [ffn_w_gate, ffn_w_up, ffn_w_down, ab_w_out, cd_w_in, cd_w_out]
    cast_later = [w.reshape(-1, w.shape[-1]) for w in stacks]

    x_lat = x.reshape(batch * n, d)
    x_ctx = ctx.reshape(batch * m, d)
    for l in range(depth):
        last = l == depth - 1
        even = l % 2 == 0
        i = l // 2
        use_ctx = (not last) or even
        if even:
            lam_init = 0.8 - 0.6 * math.exp(-0.3 * l)
            assert l == 0, "the weight-cast side job is attached to the first layer's attention"
            w_in = ab_w_in[i].astype(BF16)
            w_sp = ab_w_spatial[i].astype(BF16)
            b_sp = ab_b_spatial[i].reshape(SEC // LANE, CHUNK, 1)
            lam_tab = jnp.zeros((N_COND, LANE), F32).at[0:4, 0:QK_DIM].set(
                jnp.stack([ab_lam_q1[i], ab_lam_k1[i], ab_lam_q2[i], ab_lam_k2[i]]))
            q, k, v, u, vn = _inproj_ab(x_lat, mod3, l, _lat_row, w_in, ab_vnorm_g[i],
                                        ab_vnorm_b[i], rope_tabs)
            kc, vc = _inproj_ab(x_ctx, mod3, l, _ctx_row, w_in, ab_vnorm_g[i], ab_vnorm_b[i], None,
                                sections=(SEC_K, SEC_V), tag="_kv")
            qc, uc, vnc = _inproj_ab(x_ctx, mod3, l, _ctx_row, w_in, ab_vnorm_g[i], ab_vnorm_b[i],
                                     None, sections=(SEC_Q, SEC_U, SEC_VG), tag="_rest")
            q3, k3, v3 = (t.reshape(batch, n, SEC) for t in (q, k, v))
            kc3, vc3 = kc.reshape(batch, m, SEC), vc.reshape(batch, m, SEC)
            a_lat, cast = _diff_attention(q3, [k3, kc3], [v3, vc3], lam_tab, ab_subln_g[i], lam_init,
                                          to_cast=cast_later)
            wg, wu, wd, ab_w_out_b, cd_w_in_b, cd_w_out_b = (
                c2.reshape(w.shape) for c2, w in zip(cast, stacks))
            w_out = ab_w_out_b[i]
            x_lat_mix = _outproj_ab(a_lat.reshape(batch * n, SEC), u, vn, w_sp, b_sp, w_out, x_lat,
                                    mod3, l, _lat_row, post_mix_g[l])
            if not last:
                a_ctx, _ = _diff_attention(qc.reshape(batch, m, SEC), [kc3], [vc3], lam_tab,
                                           ab_subln_g[i], lam_init)
                x_ctx_mix = _outproj_ab(a_ctx.reshape(batch * m, SEC), uc, vnc, w_sp, b_sp, w_out,
                                        x_ctx, mod3, l, _ctx_row, post_mix_g[l])
        else:
            w_in, w_out = cd_w_in_b[i], cd_w_out_b[i]

            y, a_r, a_i = _inproj_cd(x_lat.reshape(batch, n, d), mod3, l, w_in)
            yc = _conformer_conv(y, cd_dw_w[i], cd_dw_b[i], cd_norm_g[i], cd_norm_b[i])
            yd = _fourier_mix(a_r, a_i)
            x_lat_mix = _outproj_cd(yc.reshape(-1, SEC), yd.reshape(-1, SEC), w_out, x_lat, mod3, l,
                                    _lat_row, post_mix_g[l])
            if use_ctx:
                raise NotImplementedError("odd non-final layers are outside this problem's depth")
        x_lat = _ffn(x_lat_mix, mod3, l, _lat_row, post_ffn_g[l], wg, wu, wd)
        if not last:
            x_ctx = _ffn(x_ctx_mix, mod3, l, _ctx_row, post_ffn_g[l], wg, wu, wd)
    return x_lat.reshape(batch, n, d)
```
